```python
import math
import jax
import jax.numpy as jnp
from jax import lax
import numpy as np

D_MODEL = 1024
BATCH = 2
SEQ = 16384
DEPTH = 1

GRID_W = 64
CTX_LEN = 256
RMS_EPS = 1e-6

D_MIX = 2 * D_MODEL
HEAD_DIM = 64
SSD_WIDTH = D_MIX // 2
SSD_HEADS = SSD_WIDTH // HEAD_DIM
SSD_GROUPS = 4
SSD_HPG = SSD_HEADS // SSD_GROUPS
SSD_STATE = 128
SSD_CONV = 5
SSD_CHUNK = 128
SSD_CONV_DIM = SSD_WIDTH + 2 * SSD_GROUPS * SSD_STATE
HY_WIDTH = D_MIX - SSD_WIDTH
HY_HEADS = HY_WIDTH // HEAD_DIM
HY_ORDER = 2
HY_SHORT = 3
HY_EMB = 33
HY_BANDS = (HY_EMB - 1) // 2
HY_HIDDEN = 64
HY_TARGET = 1e-2
HY_FAST_DECAY = 0.3
HY_SLOW_DECAY = 1.5
HY_MAX_DECAY = math.log(HY_TARGET) / HY_FAST_DECAY
HY_MIN_DECAY = math.log(HY_TARGET) / HY_SLOW_DECAY
PEER_HEADS = 8
PEER_KEYS = 128
PEER_EXPERTS = PEER_KEYS * PEER_KEYS
PEER_TOPK = 16
PEER_DKEY = 256
PEER_BLOCK = 128
OFF_XBC = SSD_WIDTH
OFF_DT = OFF_XBC + SSD_CONV_DIM
OFF_HY = OFF_DT + 2 * SSD_HEADS
D_IN_PROJ = OFF_HY + (HY_ORDER + 1) * HY_WIDTH

kernel_name = 'hymba_ssd_hyena_peer_block'


def rmsnorm(x, g):
    xf = x.astype(jnp.float32)
    y = xf * lax.rsqrt(jnp.mean(xf * xf, axis=-1, keepdims=True) + RMS_EPS)
    return (y * g.astype(jnp.float32)).astype(x.dtype)


def group_rmsnorm(x, g, groups):
    shp = x.shape
    xg = x.reshape(shp[:-1] + (groups, shp[-1] // groups))
    return rmsnorm(xg, g.reshape(groups, shp[-1] // groups)).reshape(shp)


def modulate(x, shift, scale):
    return x * (1.0 + scale) + shift


def dwconv_centred(u, w, b):
    k = w.shape[0]
    y = lax.conv_general_dilated(u, w[:, None, :].astype(u.dtype), window_strides=(1,),
                                 padding=[(k // 2, k // 2)],
                                 dimension_numbers=('NWC', 'WIO', 'NWC'),
                                 feature_group_count=u.shape[-1])
    return y + b


def segsum(a):
    t = a.shape[-1]
    cs = jnp.cumsum(a, axis=-1)
    diff = cs[..., :, None] - cs[..., None, :]
    mask = jnp.tril(jnp.ones((t, t), dtype=bool))
    return jnp.where(mask, diff, -jnp.inf)


def ssd_scan(x, dt, a_head, b_mat, c_mat, h0):
    bsz, seqlen, g, r, p = x.shape
    n = b_mat.shape[-1]
    q = SSD_CHUNK
    nc = seqlen // q
    xdt = (x * dt[..., None]).reshape(bsz, nc, q, g, r, p)
    bc = b_mat.reshape(bsz, nc, q, g, n)
    cc = c_mat.reshape(bsz, nc, q, g, n)
    a = jnp.moveaxis((dt * a_head).reshape(bsz, nc, q, g, r), (3, 4), (1, 2))
    a_cum = jnp.cumsum(a, axis=-1)
    l_intra = jnp.exp(segsum(a))
    y_diag = jnp.einsum('bclgn,bcsgn,bgrcls,bcsgrp->bclgrp', cc, bc, l_intra, xdt)
    decay_to_end = jnp.exp(a_cum[..., -1:] - a_cum)
    states = jnp.einsum('bclgn,bgrcl,bclgrp->bcgrpn', bc, decay_to_end, xdt)
    states = jnp.concatenate([h0[:, None].astype(states.dtype), states], axis=1)
    chunk_a = jnp.pad(a_cum[..., -1], ((0, 0), (0, 0), (0, 0), (1, 0)))
    decay_chunk = jnp.exp(segsum(chunk_a))
    states = jnp.einsum('bgrzc,bcgrpn->bzgrpn', decay_chunk, states)
    prev_states, final_state = states[:, :-1], states[:, -1]
    y_off = jnp.einsum('bclgn,bcgrpn,bgrcl->bclgrp', cc, prev_states, jnp.exp(a_cum))
    y = (y_diag + y_off).reshape(bsz, seqlen, g, r, p)
    return y, final_state


def ssd_inputs(proj, conv_w, conv_b, dt_bias):
    bsz, l, _ = proj.shape
    z = proj[..., :OFF_XBC]
    xbc = jax.nn.silu(dwconv_centred(proj[..., OFF_XBC:OFF_DT], conv_w, conv_b))
    dt_raw = proj[..., OFF_DT:OFF_HY]
    xs = xbc[..., :SSD_WIDTH].reshape(bsz, l, SSD_GROUPS, SSD_HPG, HEAD_DIM)
    gn = SSD_GROUPS * SSD_STATE
    bm = xbc[..., SSD_WIDTH:SSD_WIDTH + gn].reshape(bsz, l, SSD_GROUPS, SSD_STATE)
    cm = xbc[..., SSD_WIDTH + gn:].reshape(bsz, l, SSD_GROUPS, SSD_STATE)
    dt_f = jax.nn.softplus(dt_raw[..., :SSD_HEADS] + dt_bias[0]).reshape(bsz, l, SSD_GROUPS, SSD_HPG)
    dt_b = jax.nn.softplus(dt_raw[..., SSD_HEADS:] + dt_bias[1]).reshape(bsz, l, SSD_GROUPS, SSD_HPG)
    return z, xs, bm, cm, dt_f, dt_b


def ssd_bidir(xs, bm, cm, dt_f, dt_b, a_f, a_b, h0_f, h0_b):
    y_f, s_f = ssd_scan(xs, dt_f, a_f, bm, cm, h0_f)
    fl = lambda t: jnp.flip(t, axis=1)
    y_b, s_b = ssd_scan(fl(xs), fl(dt_b), a_b, fl(bm), fl(cm), h0_b)
    return y_f + fl(y_b), s_f, s_b


def ssd_output(y, xs, z, d_skip, norm_g):
    bsz, l = y.shape[:2]
    y = (y + xs * d_skip.reshape(SSD_GROUPS, SSD_HPG)[:, :, None]).reshape(bsz, l, SSD_WIDTH)
    return group_rmsnorm(y * jax.nn.silu(z), norm_g, SSD_GROUPS)


def hyena_filters(seqlen, w1, b1, w2, b2, w3, sin_freq):
    f32 = jnp.float32
    t = jnp.linspace(0.0, 1.0, seqlen, dtype=f32)[:, None]
    w_ang = 2.0 * math.pi * jnp.arange(seqlen, dtype=f32) / seqlen
    bands = jnp.linspace(1e-4, HY_BANDS - 1, HY_BANDS, dtype=f32)
    ang = w_ang[:, None] * bands[None, :]
    zpos = jnp.concatenate([t, jnp.cos(ang), -jnp.sin(ang)], axis=-1)
    h = jnp.sin(sin_freq[0] * (zpos @ w1 + b1))
    h = jnp.sin(sin_freq[1] * (h @ w2 + b2))
    h = (h @ w3).astype(f32).reshape(seqlen, HY_ORDER, 2, HY_WIDTH)
    deltas = jnp.abs(jnp.linspace(HY_MIN_DECAY, HY_MAX_DECAY, HY_WIDTH, dtype=f32))
    h = h * jnp.exp(-t * deltas)[:, None, None, :]
    return jnp.moveaxis(h, 0, 2)


def long_conv_bidir(u, h_fwd, h_bwd, bias):
    seqlen = u.shape[1]
    k = jnp.concatenate([h_fwd, jnp.zeros_like(h_fwd[:1]), h_bwd[:0:-1]], axis=0)
    k_f = jnp.fft.rfft(k, axis=0)
    u_f = jnp.fft.rfft(u.astype(jnp.float32), n=2 * seqlen, axis=1)
    y = jnp.fft.irfft(u_f * k_f, n=2 * seqlen, axis=1)[:, :seqlen]
    return (y + u.astype(jnp.float32) * bias.astype(jnp.float32)).astype(u.dtype)


def raster_to_colmajor(u, rows):
    bsz, l, ch = u.shape
    return u.reshape(bsz, rows, GRID_W, ch).transpose(0, 2, 1, 3).reshape(bsz, l, ch)


def colmajor_to_raster(u, rows):
    bsz, l, ch = u.shape
    return u.reshape(bsz, GRID_W, rows, ch).transpose(0, 2, 1, 3).reshape(bsz, l, ch)


def hyena(proj_hy, short_w, short_b, filters, filt_bias, norm_g, rows):
    u = dwconv_centred(proj_hy, short_w, short_b)
    v, x1, x2 = jnp.split(u, 3, axis=-1)
    z = x1 * long_conv_bidir(v, filters[0, 0], filters[0, 1], filt_bias[0])
    if rows is None:
        z = long_conv_bidir(z, filters[1, 0], filters[1, 1], filt_bias[1])
    else:
        z = colmajor_to_raster(
            long_conv_bidir(raster_to_colmajor(z, rows), filters[1, 0], filters[1, 1], filt_bias[1]), rows)
    return group_rmsnorm(x2 * z, norm_g, HY_HEADS)


def peer(h, wq, subkeys, u_tab, v_tab):
    bsz, l, d = h.shape
    blocks = h.reshape(-1, PEER_BLOCK, d)

    def block(hb):
        q = (hb @ wq).reshape(PEER_BLOCK, PEER_HEADS, 2, PEER_DKEY // 2)
        s = jnp.einsum('thsd,snd->thsn', q, subkeys).astype(jnp.float32)
        s1, i1 = lax.top_k(s[:, :, 0], PEER_TOPK)
        s2, i2 = lax.top_k(s[:, :, 1], PEER_TOPK)
        cand = (s1[..., :, None] + s2[..., None, :]).reshape(PEER_BLOCK, PEER_HEADS, PEER_TOPK * PEER_TOPK)
        sc, ci = lax.top_k(cand, PEER_TOPK)
        e1 = jnp.take_along_axis(i1, ci // PEER_TOPK, axis=-1)
        e2 = jnp.take_along_axis(i2, ci % PEER_TOPK, axis=-1)
        idx = e1 * PEER_KEYS + e2
        gate = jax.nn.softmax(sc, axis=-1).astype(hb.dtype)
        u_sel = u_tab[idx]
        v_sel = v_tab[idx]
        act = jax.nn.gelu(jnp.einsum('td,thkd->thk', hb, u_sel), approximate=False) * gate
        return jnp.einsum('thk,thkd->td', act, v_sel)

    return lax.map(block, blocks).reshape(bsz, l, d)


def setup_inputs(seed: int = 0) -> dict:
    key = jax.random.key(seed)
    ks = jax.random.split(key, 32)
    f32 = jnp.float32

    def nrm(k, shape, scale):
        return jax.random.normal(k, shape, f32) * scale

    def gain(k, shape):
        return 1.0 + 0.02 * jax.random.normal(k, shape, f32)

    dt0 = jnp.exp(jax.random.uniform(ks[9], (DEPTH, 2, SSD_HEADS), f32, math.log(1e-3), math.log(1e-1)))
    dt_bias = dt0 + jnp.log(-jnp.expm1(-dt0))
    a_log = jnp.log(jax.random.uniform(ks[8], (DEPTH, 2, SSD_HEADS), f32, 1.0, 16.0))
    return {
        'x': nrm(ks[0], (BATCH, SEQ, D_MODEL), 1.0),
        'c': nrm(ks[1], (BATCH, D_MODEL), 1.0),
        'ctx': nrm(ks[2], (BATCH, CTX_LEN, D_MODEL), 1.0),
        'c_ctx': nrm(ks[3], (D_MODEL,), 1.0),
        'w_mod': nrm(ks[4], (DEPTH, D_MODEL, 6 * D_MODEL), 0.5 * D_MODEL ** -0.5),
        'b_mod': nrm(ks[5], (DEPTH, 6 * D_MODEL), 0.02),
        'norm1': gain(ks[6], (DEPTH, D_MODEL)),
        'w_in': nrm(ks[7], (DEPTH, D_MODEL, D_IN_PROJ), D_MODEL ** -0.5),
        'ssd_conv_w': nrm(ks[10], (DEPTH, SSD_CONV, SSD_CONV_DIM), SSD_CONV ** -0.5),
        'ssd_conv_b': nrm(ks[11], (DEPTH, SSD_CONV_DIM), 0.02),
        'ssd_a_log': a_log,
        'ssd_dt_bias': dt_bias,
        'ssd_d': gain(ks[12], (DEPTH, SSD_HEADS)),
        'ssd_norm': gain(ks[13], (DEPTH, SSD_WIDTH)),
        'hy_short_w': nrm(ks[14], (DEPTH, HY_SHORT, (HY_ORDER + 1) * HY_WIDTH), HY_SHORT ** -0.5),
        'hy_short_b': nrm(ks[15], (DEPTH, (HY_ORDER + 1) * HY_WIDTH), 0.02),
        'hy_w1': nrm(ks[16], (DEPTH, HY_EMB, HY_HIDDEN), HY_EMB ** -0.5),
        'hy_b1': nrm(ks[17], (DEPTH, HY_HIDDEN), 0.02),
        'hy_w2': nrm(ks[18], (DEPTH, HY_HIDDEN, HY_HIDDEN), HY_HIDDEN ** -0.5),
        'hy_b2': nrm(ks[19], (DEPTH, HY_HIDDEN), 0.02),
        'hy_w3': nrm(ks[20], (DEPTH, HY_HIDDEN, HY_ORDER * 2 * HY_WIDTH), 0.05 * HY_HIDDEN ** -0.5),
        'hy_sin_freq': 1.0 + 0.1 * jax.random.normal(ks[21], (DEPTH, 2, HY_HIDDEN), f32),
        'hy_filt_bias': nrm(ks[22], (DEPTH, HY_ORDER, HY_WIDTH), 0.5),
        'hy_norm': gain(ks[23], (DEPTH, HY_WIDTH)),
        'w_out': nrm(ks[24], (DEPTH, D_MIX, D_MODEL), D_MIX ** -0.5),
        'norm2': gain(ks[25], (DEPTH, D_MODEL)),
        'peer_wq': nrm(ks[26], (DEPTH, D_MODEL, PEER_HEADS * PEER_DKEY), D_MODEL ** -0.5),
        'peer_subkeys': nrm(ks[27], (DEPTH, 2, PEER_KEYS, PEER_DKEY // 2), (PEER_DKEY // 2) ** -0.5),
        'peer_u': nrm(ks[28], (DEPTH, PEER_EXPERTS, D_MODEL), D_MODEL ** -0.5),
        'peer_v': nrm(ks[29], (DEPTH, PEER_EXPERTS, D_MODEL), 1.0),
        'final_norm': gain(ks[30], (D_MODEL,)),
    }


def reference(x, c, ctx, c_ctx, w_mod, b_mod, norm1, w_in, ssd_conv_w, ssd_conv_b, ssd_a_log,
              ssd_dt_bias, ssd_d, ssd_norm, hy_short_w, hy_short_b, hy_w1, hy_b1, hy_w2, hy_b2,
              hy_w3, hy_sin_freq, hy_filt_bias, hy_norm, w_out, norm2, peer_wq, peer_subkeys,
              peer_u, peer_v, final_norm):
    rows = x.shape[1] // GRID_W
    h_lat = x
    h_ctx = ctx
    bsz = x.shape[0]
    for i in range(DEPTH):
        last = i == DEPTH - 1
        mod_l = (jax.nn.silu(c) @ w_mod[i] + b_mod[i])[:, None, :]
        mod_c = (jax.nn.silu(c_ctx) @ w_mod[i] + b_mod[i])[None, None, :]
        sh1_l, sc1_l, g1_l, sh2_l, sc2_l, g2_l = jnp.split(mod_l, 6, axis=-1)
        sh1_c, sc1_c, g1_c, sh2_c, sc2_c, g2_c = jnp.split(mod_c, 6, axis=-1)

        p_l = modulate(rmsnorm(h_lat, norm1[i]), sh1_l, sc1_l) @ w_in[i]
        p_c = modulate(rmsnorm(h_ctx, norm1[i]), sh1_c, sc1_c) @ w_in[i]

        a_f = -jnp.exp(ssd_a_log[i, 0]).reshape(SSD_GROUPS, SSD_HPG)
        a_b = -jnp.exp(ssd_a_log[i, 1]).reshape(SSD_GROUPS, SSD_HPG)
        z_c, xs_c, b_c, c_c, dtf_c, dtb_c = ssd_inputs(p_c, ssd_conv_w[i], ssd_conv_b[i], ssd_dt_bias[i])
        z_l, xs_l, b_l, c_l, dtf_l, dtb_l = ssd_inputs(p_l, ssd_conv_w[i], ssd_conv_b[i], ssd_dt_bias[i])
        h0 = jnp.zeros((bsz, SSD_GROUPS, SSD_HPG, HEAD_DIM, SSD_STATE), dtype=xs_c.dtype)
        y_c, s_f, s_b = ssd_bidir(xs_c, b_c, c_c, dtf_c, dtb_c, a_f, a_b, h0, h0)
        y_l, _, _ = ssd_bidir(xs_l, b_l, c_l, dtf_l, dtb_l, a_f, a_b, s_f, s_b)
        o_ssd_l = ssd_output(y_l, xs_l, z_l, ssd_d[i], ssd_norm[i])

        filt_l = hyena_filters(h_lat.shape[1], hy_w1[i], hy_b1[i], hy_w2[i], hy_b2[i], hy_w3[i], hy_sin_freq[i])
        o_hy_l = hyena(p_l[..., OFF_HY:], hy_short_w[i], hy_short_b[i], filt_l, hy_filt_bias[i], hy_norm[i], rows)

        mix_l = jnp.concatenate([o_ssd_l, o_hy_l], axis=-1) @ w_out[i]
        h_lat = h_lat + g1_l * mix_l

        f_l = modulate(rmsnorm(h_lat, norm2[i]), sh2_l, sc2_l)
        h_lat = h_lat + g2_l * peer(f_l, peer_wq[i], peer_subkeys[i], peer_u[i], peer_v[i])

        if not last:
            o_ssd_c = ssd_output(y_c, xs_c, z_c, ssd_d[i], ssd_norm[i])
            filt_c = hyena_filters(h_ctx.shape[1], hy_w1[i], hy_b1[i], hy_w2[i], hy_b2[i], hy_w3[i], hy_sin_freq[i])
            o_hy_c = hyena(p_c[..., OFF_HY:], hy_short_w[i], hy_short_b[i], filt_c, hy_filt_bias[i], hy_norm[i], None)
            h_ctx = h_ctx + g1_c * (jnp.concatenate([o_ssd_c, o_hy_c], axis=-1) @ w_out[i])
            f_c = modulate(rmsnorm(h_ctx, norm2[i]), sh2_c, sc2_c)
            h_ctx = h_ctx + g2_c * peer(f_c, peer_wq[i], peer_subkeys[i], peer_u[i], peer_v[i])
    return rmsnorm(h_lat, final_norm)
```

```python
import functools
import math

import jax
import jax.numpy as jnp
from jax import lax
from jax.experimental import pallas as pl
from jax.experimental.pallas import tpu as pltpu

F32 = jnp.float32
BF16 = jnp.bfloat16
HIGHEST = lax.Precision.HIGHEST

LANE = 128
VMEM_LIMIT = 48 * 1024 * 1024

RMS_EPS = 1e-6
GRID_W = 64
HEAD_DIM = 64
SSD_GROUPS = 4
SSD_HPG = 4
SSD_HEADS = SSD_GROUPS * SSD_HPG
SSD_STATE = 128
SSD_CHUNK = 128
SSD_CONV = 5
HY_SHORT = 3
HY_EMB = 33
HY_BANDS = (HY_EMB - 1) // 2
HY_TARGET = 1e-2
HY_MAX_DECAY = math.log(HY_TARGET) / 0.3
HY_MIN_DECAY = math.log(HY_TARGET) / 1.5
PEER_HEADS = 8
PEER_KEYS = 128
PEER_TOPK = 16
PEER_HALF = 128
FFT_N2 = 128
HALO = 16


def _cparams(*sem):
    return pltpu.CompilerParams(dimension_semantics=sem, vmem_limit_bytes=VMEM_LIMIT)


def _silu(x):
    return x * jax.nn.sigmoid(x)


def _mod_kernel(c_ref, w_ref, b_ref, o_ref):
    s = _silu(c_ref[...])
    o_ref[...] = jnp.dot(s, w_ref[...], precision=HIGHEST, preferred_element_type=F32) + b_ref[...]


def modulation(c_rows, w_mod, b_mod):
    d, n = w_mod.shape
    tn = 512
    return pl.pallas_call(
        _mod_kernel,
        grid=(n // tn,),
        in_specs=[pl.BlockSpec((8, d), lambda j: (0, 0)),
                  pl.BlockSpec((d, tn), lambda j: (0, j)),
                  pl.BlockSpec((1, tn), lambda j: (0, j))],
        out_specs=pl.BlockSpec((8, tn), lambda j: (0, j)),
        out_shape=jax.ShapeDtypeStruct((8, n), F32),
        compiler_params=_cparams("arbitrary"),
        name="modulation",
    )(c_rows, w_mod, b_mod.reshape(1, n))


def _inproj_kernel(x_ref, sh_ref, sc_ref, g_ref, wz_ref, wx_ref, wd_ref, wh_ref,
                   z_ref, xbc_ref, dt_ref, hy_ref):
    x = x_ref[0]
    xn = x * lax.rsqrt(jnp.mean(x * x, axis=-1, keepdims=True) + RMS_EPS) * g_ref[...]
    xm = (xn * (1.0 + sc_ref[0]) + sh_ref[0]).astype(BF16)
    z_ref[0] = jnp.dot(xm, wz_ref[...], preferred_element_type=F32).astype(z_ref.dtype)
    xbc_ref[0] = jnp.dot(xm, wx_ref[...], preferred_element_type=F32).astype(xbc_ref.dtype)
    dt_ref[0] = jnp.dot(xm, wd_ref[...], preferred_element_type=F32)
    hy_ref[0] = jnp.dot(xm, wh_ref[...], preferred_element_type=F32).astype(hy_ref.dtype)


def in_projection(x, shift, scale, gain, wz, wx, wd, wh, tm):
    b, s, d = x.shape
    const = lambda w: pl.BlockSpec(w.shape, lambda i, j: (0, 0), pipeline_mode=pl.Buffered(1))
    row = lambda n: pl.BlockSpec((1, tm, n), lambda i, j: (i, j, 0))
    vec = pl.BlockSpec((1, 1, d), lambda i, j: (i, 0, 0))
    return pl.pallas_call(
        _inproj_kernel,
        grid=(b, s // tm),
        in_specs=[row(d), vec, vec, pl.BlockSpec((1, d), lambda i, j: (0, 0)),
                  const(wz), const(wx), const(wd), const(wh)],
        out_specs=[row(wz.shape[1]), row(wx.shape[1]), row(wd.shape[1]), row(wh.shape[1])],
        out_shape=[jax.ShapeDtypeStruct((b, s, wz.shape[1]), BF16),
                   jax.ShapeDtypeStruct((b, s, wx.shape[1]), BF16),
                   jax.ShapeDtypeStruct((b, s, wd.shape[1]), F32),
                   jax.ShapeDtypeStruct((b, s, wh.shape[1]), BF16)],
        compiler_params=_cparams("parallel", "parallel"),
        name="in_projection",
    )(x, shift, scale, gain.reshape(1, d), wz, wx, wd, wh)


def _dwconv_kernel(prev_ref, main_ref, next_ref, w_ref, b_ref, o_ref, ext_ref, *, taps, act, tm, nt):
    i = pl.program_id(1)
    ext_ref[pl.ds(0, HALO), :] = jnp.where(i > 0, prev_ref[0].astype(F32), 0.0)
    ext_ref[pl.ds(HALO, tm), :] = main_ref[0].astype(F32)
    ext_ref[pl.ds(HALO + tm, HALO), :] = jnp.where(i < nt - 1, next_ref[0].astype(F32), 0.0)
    acc = jnp.broadcast_to(b_ref[...], (tm, b_ref.shape[1]))
    for k in range(taps):
        acc = acc + w_ref[pl.ds(k, 1), :] * ext_ref[pl.ds(HALO - taps // 2 + k, tm), :]
    if act:
        acc = _silu(acc)
    o_ref[0] = acc.astype(o_ref.dtype)


def dwconv(u, w, bias, col0, ncols, act, out_dtype, tm, cb=512):
    b, s, _ = u.shape
    taps = w.shape[0]
    nt = s // tm
    hb = tm // HALO
    c0 = col0 // cb
    w8 = jnp.zeros((8, ncols), F32).at[:taps].set(w)
    kern = functools.partial(_dwconv_kernel, taps=taps, act=act, tm=tm, nt=nt)
    return pl.pallas_call(
        kern,
        grid=(b, nt, ncols // cb),
        in_specs=[pl.BlockSpec((1, HALO, cb), lambda bi, i, j: (bi, jnp.maximum(i * hb - 1, 0), j + c0)),
                  pl.BlockSpec((1, tm, cb), lambda bi, i, j: (bi, i, j + c0)),
                  pl.BlockSpec((1, HALO, cb), lambda bi, i, j: (bi, jnp.minimum((i + 1) * hb, s // HALO - 1), j + c0)),
                  pl.BlockSpec((8, cb), lambda bi, i, j: (0, j)),
                  pl.BlockSpec((1, cb), lambda bi, i, j: (0, j))],
        out_specs=pl.BlockSpec((1, tm, cb), lambda bi, i, j: (bi, i, j)),
        out_shape=jax.ShapeDtypeStruct((b, s, ncols), out_dtype),
        scratch_shapes=[pltpu.VMEM((tm + 2 * HALO, cb), F32)],
        compiler_params=_cparams("parallel", "parallel", "parallel"),
        name="dwconv",
    )(u, u, u, w8, bias.reshape(1, ncols))


def _ssd_kernel(x_ref, b_ref, c_ref, dt_ref, bias_ref, a_ref, dskip_ref, h0_ref,
                y_ref, st_ref, *, rev, col0):
    q = SSD_CHUNK
    gw = SSD_HPG * HEAD_DIM

    @pl.when(pl.program_id(1) == 0)
    def _():
        st_ref[...] = h0_ref[...]

    dtp = jax.nn.softplus(dt_ref[0] + bias_ref[...])
    a = dtp * a_ref[...]
    li = lax.broadcasted_iota(jnp.int32, (q, q), 0)
    si = lax.broadcasted_iota(jnp.int32, (q, q), 1)
    tri = (si <= li).astype(F32)
    cum = jnp.dot(tri, a, precision=HIGHEST, preferred_element_type=F32)
    total = cum[q - 1:q, :]
    if rev:
        r = a - cum
        keep = si >= li
        d_out = jnp.exp(total + r)
        d_st = jnp.exp(-r)
    else:
        r = cum
        keep = li >= si
        d_out = jnp.exp(r)
        d_st = jnp.exp(total - r)
    r_t = jnp.transpose(r)
    d_tot = jnp.exp(total)
    lane_head = lax.broadcasted_iota(jnp.int32, (1, gw), 1) // HEAD_DIM

    x = x_ref[0].astype(F32)
    for g in range(SSD_GROUPS):
        cg = c_ref[0, :, g * SSD_STATE:(g + 1) * SSD_STATE]
        bg = b_ref[0, :, g * SSD_STATE:(g + 1) * SSD_STATE]
        cb = lax.dot_general(cg, bg, (((1,), (1,)), ((), ())), preferred_element_type=F32)
        xg = x[:, g * gw:(g + 1) * gw]
        dt_g = jnp.zeros((q, gw), F32)
        dout_g = jnp.zeros((q, gw), F32)
        dst_g = jnp.zeros((q, gw), F32)
        dtot_g = jnp.zeros((1, gw), F32)
        for hh in range(SSD_HPG):
            hc = col0 + g * SSD_HPG + hh
            m = lane_head == hh
            dt_g = jnp.where(m, dtp[:, hc:hc + 1], dt_g)
            dout_g = jnp.where(m, d_out[:, hc:hc + 1], dout_g)
            dst_g = jnp.where(m, d_st[:, hc:hc + 1], dst_g)
            dtot_g = jnp.where(m, d_tot[:, hc:hc + 1], dtot_g)
        xdt = xg * dt_g
        st_old = st_ref[0, g]
        y_g = jnp.dot(cg, st_old.astype(BF16), preferred_element_type=F32) * dout_g
        for hh in range(SSD_HPG):
            hc = col0 + g * SSD_HPG + hh
            diff = r[:, hc:hc + 1] - r_t[hc:hc + 1, :]
            lm = jnp.where(keep, jnp.exp(jnp.minimum(diff, 0.0)), 0.0)
            mh = (cb * lm).astype(BF16)
            xh = jnp.where(lane_head == hh, xdt, 0.0).astype(BF16)
            y_g = y_g + jnp.dot(mh, xh, preferred_element_type=F32)
        if not rev:
            y_g = y_g + xg * dskip_ref[:, g * gw:(g + 1) * gw]
        y_ref[0, :, g * gw:(g + 1) * gw] = y_g.astype(y_ref.dtype)
        upd = lax.dot_general(bg, (xdt * dst_g).astype(BF16), (((0,), (0,)), ((), ())),
                              preferred_element_type=F32)
        st_ref[0, g] = st_old * dtot_g + upd


def ssd_scan(xbc, dt_raw, dt_bias_row, a_row, dskip_row, h0, rev):
    b, s, _ = xbc.shape
    w = SSD_HEADS * HEAD_DIM
    gn = SSD_GROUPS * SSD_STATE
    nc = s // SSD_CHUNK
    cidx = (lambda c: nc - 1 - c) if rev else (lambda c: c)
    kern = functools.partial(_ssd_kernel, rev=rev, col0=SSD_HEADS if rev else 0)
    st_spec = pl.BlockSpec((1, SSD_GROUPS, SSD_STATE, SSD_HPG * HEAD_DIM), lambda bi, c: (bi, 0, 0, 0))
    return pl.pallas_call(
        kern,
        grid=(b, nc),
        in_specs=[pl.BlockSpec((1, SSD_CHUNK, w), lambda bi, c: (bi, cidx(c), 0)),
                  pl.BlockSpec((1, SSD_CHUNK, gn), lambda bi, c: (bi, cidx(c), w // gn)),
                  pl.BlockSpec((1, SSD_CHUNK, gn), lambda bi, c: (bi, cidx(c), w // gn + 1)),
                  pl.BlockSpec((1, SSD_CHUNK, LANE), lambda bi, c: (bi, cidx(c), 0)),
                  pl.BlockSpec((1, LANE), lambda bi, c: (0, 0)),
                  pl.BlockSpec((1, LANE), lambda bi, c: (0, 0)),
                  pl.BlockSpec((1, w), lambda bi, c: (0, 0)),
                  st_spec],
        out_specs=[pl.BlockSpec((1, SSD_CHUNK, w), lambda bi, c: (bi, cidx(c), 0)), st_spec],
        out_shape=[jax.ShapeDtypeStruct((b, s, w), BF16),
                   jax.ShapeDtypeStruct(h0.shape, F32)],
        compiler_params=_cparams("parallel", "arbitrary"),
        name="ssd_scan_rev" if rev else "ssd_scan_fwd",
    )(xbc, xbc, xbc, dt_raw, dt_bias_row, a_row, dskip_row, h0)


def _filter_kernel(z_ref, w1_ref, b1_ref, w2_ref, b2_ref, w3_ref, f0_ref, f1_ref, dl_ref, o_ref):
    z = z_ref[...]
    h = jnp.sin(f0_ref[...] * (jnp.dot(z, w1_ref[...], precision=HIGHEST, preferred_element_type=F32) + b1_ref[...]))
    h = jnp.sin(f1_ref[...] * (jnp.dot(h, w2_ref[...], precision=HIGHEST, preferred_element_type=F32) + b2_ref[...]))
    h = jnp.dot(h, w3_ref[...], precision=HIGHEST, preferred_element_type=F32)
    o_ref[...] = h * jnp.exp(-z[:, 0:1] * dl_ref[...])


def hyena_filters(seqlen, w1, b1, w2, b2, w3, sin_freq, width):
    t = jnp.linspace(0.0, 1.0, seqlen, dtype=F32)[:, None]
    w_ang = 2.0 * math.pi * jnp.arange(seqlen, dtype=F32) / seqlen
    bands = jnp.linspace(1e-4, HY_BANDS - 1, HY_BANDS, dtype=F32)
    ang = w_ang[:, None] * bands[None, :]
    zpos = jnp.concatenate([t, jnp.cos(ang), -jnp.sin(ang)], axis=-1)
    zpad = jnp.zeros((seqlen, LANE), F32).at[:, :HY_EMB].set(zpos)
    hid = w1.shape[1]
    w1p = jnp.zeros((LANE, hid), F32).at[:HY_EMB].set(w1)
    deltas = jnp.abs(jnp.linspace(HY_MIN_DECAY, HY_MAX_DECAY, width, dtype=F32))
    dl = jnp.tile(deltas, 4).reshape(1, 4 * width)
    tl = 512
    tn = 1024
    full = lambda a: pl.BlockSpec(a.shape, lambda i, j: (0, 0))
    return pl.pallas_call(
        _filter_kernel,
        grid=(seqlen // tl, 4 * width // tn),
        in_specs=[pl.BlockSpec((tl, LANE), lambda i, j: (i, 0)),
                  full(w1p), pl.BlockSpec((1, hid), lambda i, j: (0, 0)),
                  full(w2), pl.BlockSpec((1, hid), lambda i, j: (0, 0)),
                  pl.BlockSpec((hid, tn), lambda i, j: (0, j)),
                  pl.BlockSpec((1, hid), lambda i, j: (0, 0)),
                  pl.BlockSpec((1, hid), lambda i, j: (0, 0)),
                  pl.BlockSpec((1, tn), lambda i, j: (0, j))],
        out_specs=pl.BlockSpec((tl, tn), lambda i, j: (i, j)),
        out_shape=jax.ShapeDtypeStruct((seqlen, 4 * width), F32),
        compiler_params=_cparams("parallel", "parallel"),
        name="hyena_filters",
    )(zpad, w1p, b1.reshape(1, hid), w2, b2.reshape(1, hid), w3,
      sin_freq[0].reshape(1, hid), sin_freq[1].reshape(1, hid), dl)


def _dft_tables(seqlen, colmajor):
    n = 2 * seqlen
    n2 = FFT_N2
    n1 = n // n2
    h1 = n1 // 2
    k1 = jnp.arange(n1, dtype=jnp.int32)
    m1 = jnp.arange(h1, dtype=jnp.int32)
    if colmajor:
        m1 = 2 * (m1 % GRID_W) + m1 // GRID_W
    ph = (k1[:, None] * m1[None, :] * n2) % n
    ang = (-2.0 * math.pi / n) * ph.astype(F32)
    wr, wi = jnp.cos(ang), jnp.sin(ang)
    w_fwd = jnp.concatenate([jnp.concatenate([wr, -wi], 1), jnp.concatenate([wi, wr], 1)], 0)
    w_inv = jnp.concatenate([jnp.concatenate([wr.T, wi.T], 1), jnp.concatenate([-wi.T, wr.T], 1)], 0) / n
    mf = jnp.arange(n1, dtype=jnp.int32)
    phf = (k1[:, None] * mf[None, :] * n2) % n
    angf = (-2.0 * math.pi / n) * phf.astype(F32)
    w_flt = jnp.concatenate([jnp.cos(angf), jnp.sin(angf)], 0)
    a2 = jnp.arange(n2, dtype=jnp.int32)
    ph2 = (a2[None, :, None] * a2[None, None, :] * n1 + k1[:, None, None] * a2[None, None, :]) % n
    ang2 = (-2.0 * math.pi / n) * ph2.astype(F32)
    c2, s2 = jnp.cos(ang2), jnp.sin(ang2)
    w2_fwd = jnp.concatenate([jnp.concatenate([c2, -s2], 2), jnp.concatenate([s2, c2], 2)], 1)
    c2t, s2t = jnp.swapaxes(c2, 1, 2), jnp.swapaxes(s2, 1, 2)
    w2_inv = jnp.concatenate([jnp.concatenate([c2t, s2t], 2), jnp.concatenate([-s2t, c2t], 2)], 1)
    return (w_fwd.astype(BF16), w_inv.astype(BF16), w_flt.astype(BF16),
            w2_fwd.astype(BF16), w2_inv.astype(BF16))


def _fft_outer_fwd_kernel(x_ref, w_ref, o_ref, *, colmajor, nb):
    n1 = o_ref.shape[0]
    for j in range(8):
        parts = []
        for bi in range(nb):
            if colmajor:
                parts.append(x_ref[bi, 0, pl.ds(j * GRID_W, GRID_W), :].astype(F32))
                parts.append(x_ref[bi, 1, pl.ds(j * GRID_W, GRID_W), :].astype(F32))
            else:
                parts.append(x_ref[bi, :, j, :])
        xs = jnp.concatenate(parts, axis=0).astype(BF16)
        a = jnp.dot(w_ref[...], xs, preferred_element_type=F32)
        o_ref[:, 0, j, :] = a[:n1]
        o_ref[:, 1, j, :] = a[n1:]


def fft_outer_fwd(x, w, colmajor, cblk=512):
    b, seqlen, c = x.shape
    n2 = FFT_N2
    n1 = 2 * seqlen // n2
    if colmajor:
        xv = x.reshape(b, 2, seqlen // 2, c)
        in_spec = pl.BlockSpec((b, 2, 8 * GRID_W, cblk), lambda g, j: (0, 0, g, j))
    else:
        xv = x.reshape(b, n1 // 2, n2, c)
        in_spec = pl.BlockSpec((b, n1 // 2, 8, cblk), lambda g, j: (0, 0, g, j))
    kern = functools.partial(_fft_outer_fwd_kernel, colmajor=colmajor, nb=b)
    return pl.pallas_call(
        kern,
        grid=(n2 // 8, c // cblk),
        in_specs=[in_spec, pl.BlockSpec(w.shape, lambda g, j: (0, 0))],
        out_specs=pl.BlockSpec((n1, 2, 8, cblk), lambda g, j: (0, 0, g, j)),
        out_shape=jax.ShapeDtypeStruct((n1, 2, n2, c), F32),
        compiler_params=_cparams("parallel", "parallel"),
        name="fft_outer_fwd_cm" if colmajor else "fft_outer_fwd",
    )(xv, w)


def _fft_filter_outer_kernel(k_ref, w_ref, o_ref):
    n1 = o_ref.shape[0]
    for j in range(8):
        a = jnp.dot(w_ref[...], k_ref[:, j, :].astype(BF16), preferred_element_type=F32)
        o_ref[:, 0, j, :] = a[:n1]
        o_ref[:, 1, j, :] = a[n1:]


def fft_filter_outer(k, w, cblk=512):
    n, c = k.shape
    n2 = FFT_N2
    n1 = n // n2
    return pl.pallas_call(
        _fft_filter_outer_kernel,
        grid=(n2 // 8, c // cblk),
        in_specs=[pl.BlockSpec((n1, 8, cblk), lambda g, j: (0, g, j)),
                  pl.BlockSpec(w.shape, lambda g, j: (0, 0))],
        out_specs=pl.BlockSpec((n1, 2, 8, cblk), lambda g, j: (0, 0, g, j)),
        out_shape=jax.ShapeDtypeStruct((n1, 2, n2, c), F32),
        compiler_params=_cparams("parallel", "parallel"),
        name="fft_filter_outer",
    )(k.reshape(n1, n2, c), w)


def _fft_inner_spec_kernel(a_ref, wf_ref, o_ref):
    n2 = FFT_N2
    cb = a_ref.shape[-1]
    for j in range(8):
        xs = a_ref[j].reshape(2 * n2, cb).astype(BF16)
        o_ref[j] = jnp.dot(wf_ref[j], xs, preferred_element_type=F32).reshape(2, n2, cb)


def fft_inner_spectrum(a, w2_fwd, cblk=512):
    n1, _, n2, c = a.shape
    blk = pl.BlockSpec((8, 2, n2, cblk), lambda g, j: (g, 0, 0, j))
    return pl.pallas_call(
        _fft_inner_spec_kernel,
        grid=(n1 // 8, c // cblk),
        in_specs=[blk, pl.BlockSpec((8, 2 * n2, 2 * n2), lambda g, j: (g, 0, 0))],
        out_specs=blk,
        out_shape=jax.ShapeDtypeStruct(a.shape, F32),
        compiler_params=_cparams("parallel", "arbitrary"),
        name="fft_inner_spectrum",
    )(a, w2_fwd)


def _fft_inner_conv_kernel(a_ref, k_ref, wf_ref, wi_ref, o_ref):
    n2 = FFT_N2
    cb = a_ref.shape[-1]
    for j in range(8):
        xs = a_ref[j].reshape(2 * n2, cb).astype(BF16)
        s = jnp.dot(wf_ref[j], xs, preferred_element_type=F32)
        sr, si = s[:n2], s[n2:]
        kr, ki = k_ref[j, 0], k_ref[j, 1]
        y = jnp.concatenate([sr * kr - si * ki, sr * ki + si * kr], axis=0).astype(BF16)
        z = jnp.dot(wi_ref[j], y, preferred_element_type=F32)
        o_ref[:, 0, j, :] = z[:n2]
        o_ref[:, 1, j, :] = z[n2:]


def fft_inner_conv(a, kspec, kcol0, w2_fwd, w2_inv, cblk=512):
    n1, _, n2, c = a.shape
    kc = kcol0 // cblk
    wspec = pl.BlockSpec((8, 2 * n2, 2 * n2), lambda g, j: (g, 0, 0))
    return pl.pallas_call(
        _fft_inner_conv_kernel,
        grid=(n1 // 8, c // cblk),
        in_specs=[pl.BlockSpec((8, 2, n2, cblk), lambda g, j: (g, 0, 0, j)),
                  pl.BlockSpec((8, 2, n2, cblk), lambda g, j: (g, 0, 0, j + kc)),
                  wspec, wspec],
        out_specs=pl.BlockSpec((n2, 2, 8, cblk), lambda g, j: (0, 0, g, j)),
        out_shape=jax.ShapeDtypeStruct((n2, 2, n1, c), F32),
        compiler_params=_cparams("parallel", "arbitrary"),
        name="fft_inner_conv",
    )(a, kspec, w2_fwd, w2_inv)


def _fft_outer_inv_kernel(b_ref, w_ref, o_ref, *, colmajor, nb):
    cb = b_ref.shape[-1]
    n1 = b_ref.shape[2]
    h1 = n1 // 2
    for j in range(8):
        xs = b_ref[j].reshape(2 * n1, cb).astype(BF16)
        y = jnp.dot(w_ref[...], xs, preferred_element_type=F32)
        for bi in range(nb):
            yb = y[bi * h1:(bi + 1) * h1]
            if colmajor:
                o_ref[bi, 0, pl.ds(j * GRID_W, GRID_W), :] = yb[:GRID_W]
                o_ref[bi, 1, pl.ds(j * GRID_W, GRID_W), :] = yb[GRID_W:]
            else:
                o_ref[bi, :, j, :] = yb


def fft_outer_inv(bsp, w, nb, colmajor, cblk=512):
    n2, _, n1, c = bsp.shape
    seqlen = n1 * n2 // 2
    if colmajor:
        out_spec = pl.BlockSpec((nb, 2, 8 * GRID_W, cblk), lambda g, j: (0, 0, g, j))
        out_shape = jax.ShapeDtypeStruct((nb, 2, seqlen // 2, c), F32)
    else:
        out_spec = pl.BlockSpec((nb, n1 // 2, 8, cblk), lambda g, j: (0, 0, g, j))
        out_shape = jax.ShapeDtypeStruct((nb, n1 // 2, n2, c), F32)
    kern = functools.partial(_fft_outer_inv_kernel, colmajor=colmajor, nb=nb)
    y = pl.pallas_call(
        kern,
        grid=(n2 // 8, c // cblk),
        in_specs=[pl.BlockSpec((8, 2, n1, cblk), lambda g, j: (g, 0, 0, j)),
                  pl.BlockSpec(w.shape, lambda g, j: (0, 0))],
        out_specs=out_spec,
        out_shape=out_shape,
        compiler_params=_cparams("parallel", "parallel"),
        name="fft_outer_inv_cm" if colmajor else "fft_outer_inv",
    )(bsp, w)
    return y.reshape(nb, seqlen, c)


def long_conv(u, kspec, kcol0, tables, colmajor):
    w_fwd, w_inv, _, w2_fwd, w2_inv = tables
    a = fft_outer_fwd(u, w_fwd, colmajor)
    bsp = fft_inner_conv(a, kspec, kcol0, w2_fwd, w2_inv)
    return fft_outer_inv(bsp, w_inv, u.shape[0], colmajor)


def _hy_gate_kernel(y_ref, v_ref, x1_ref, fb_ref, o_ref):
    o_ref[0] = x1_ref[0].astype(F32) * (y_ref[0] + v_ref[0] * fb_ref[...])


def hyena_gate(y, v, x12, fbias, tm=512, cb=512):
    b, s, c = y.shape
    blk = lambda: pl.BlockSpec((1, tm, cb), lambda bi, i, j: (bi, i, j))
    return pl.pallas_call(
        _hy_gate_kernel,
        grid=(b, s // tm, c // cb),
        in_specs=[blk(), blk(), blk(), pl.BlockSpec((1, cb), lambda bi, i, j: (0, j))],
        out_specs=blk(),
        out_shape=jax.ShapeDtypeStruct((b, s, c), F32),
        compiler_params=_cparams("parallel", "parallel", "parallel"),
        name="hyena_gate",
    )(y, v, x12, fbias.reshape(1, c))


def _mix_kernel(x_ref, yf_ref, yb_ref, z_ref, y2_ref, zz_ref, x12_ref,
                fb_ref, gs_ref, gh_ref, mh_ref, wo_ref, g1_ref, n2_ref, sh2_ref, sc2_ref,
                wq_ref, sk_ref, h1_ref, flo_ref, fhi_ref, st_ref):
    w = yf_ref.shape[-1]
    gw = w // SSD_GROUPS
    ys = (yf_ref[0].astype(F32) + yb_ref[0].astype(F32)) * _silu(z_ref[0].astype(F32))
    parts = []
    for g in range(SSD_GROUPS):
        blk = ys[:, g * gw:(g + 1) * gw]
        ms = jnp.mean(blk * blk, axis=-1, keepdims=True)
        parts.append(blk * lax.rsqrt(ms + RMS_EPS))
    o_s = jnp.concatenate(parts, axis=1) * gs_ref[...]
    yh = x12_ref[0].astype(F32) * (y2_ref[0] + zz_ref[0] * fb_ref[...])
    parts = []
    for c in range(w // LANE):
        blk = yh[:, c * LANE:(c + 1) * LANE]
        sq = blk * blk
        hi = sq.astype(BF16)
        lo = (sq - hi.astype(F32)).astype(BF16)
        ms = (jnp.dot(hi, mh_ref[...], preferred_element_type=F32)
              + jnp.dot(lo, mh_ref[...], preferred_element_type=F32)) * (1.0 / HEAD_DIM)
        parts.append(blk * lax.rsqrt(ms + RMS_EPS))
    o_h = jnp.concatenate(parts, axis=1) * gh_ref[...]
    mix = (jnp.dot(o_s.astype(BF16), wo_ref[pl.ds(0, w), :], preferred_element_type=F32)
           + jnp.dot(o_h.astype(BF16), wo_ref[pl.ds(w, w), :], preferred_element_type=F32))
    h1 = x_ref[0] + g1_ref[0] * mix
    h1_ref[0] = h1
    hn = h1 * lax.rsqrt(jnp.mean(h1 * h1, axis=-1, keepdims=True) + RMS_EPS) * n2_ref[...]
    f = hn * (1.0 + sc2_ref[0]) + sh2_ref[0]
    half = f.shape[-1] // 2
    flo_ref[0] = f[:, :half]
    fhi_ref[0] = f[:, half:]
    qv = jnp.dot(f.astype(BF16), wq_ref[...], preferred_element_type=F32).astype(BF16)
    for blk in range(2 * PEER_HEADS):
        qb = qv[:, blk * PEER_HALF:(blk + 1) * PEER_HALF]
        st_ref[blk] = lax.dot_general(sk_ref[blk % 2], qb, (((1,), (1,)), ((), ())),
                                      preferred_element_type=F32)


def mixer_output(x, y_f, y_b, z, y2, zz, x12, fbias, g_ssd, g_hy, w_out, g1, norm2, sh2, sc2, wq, subkeys, tm=256):
    b, s, d = x.shape
    w = y_f.shape[-1]
    nblk = 2 * PEER_HEADS
    lane = jnp.arange(LANE)
    m_h = (lane[:, None] // HEAD_DIM == lane[None, :] // HEAD_DIM).astype(BF16)
    row = lambda n, c0=0: pl.BlockSpec((1, tm, n), lambda i, j: (i, j, c0))
    vec = lambda n: pl.BlockSpec((1, 1, n), lambda i, j: (i, 0, 0))
    const = lambda a: pl.BlockSpec(a.shape, lambda i, j: (0,) * a.ndim, pipeline_mode=pl.Buffered(1))
    sk = subkeys.astype(BF16)
    return pl.pallas_call(
        _mix_kernel,
        grid=(b, s // tm),
        in_specs=[row(d), row(w), row(w), row(w), row(w), row(w), row(w, 1),
                  pl.BlockSpec((1, w), lambda i, j: (0, 0)),
                  pl.BlockSpec((1, w), lambda i, j: (0, 0)),
                  pl.BlockSpec((1, w), lambda i, j: (0, 0)),
                  const(m_h), const(w_out), vec(d),
                  pl.BlockSpec((1, d), lambda i, j: (0, 0)), vec(d), vec(d),
                  const(wq), const(sk)],
        out_specs=[row(d), row(d // 2), row(d // 2),
                   pl.BlockSpec((nblk, PEER_KEYS, tm), lambda i, j: (0, 0, i * (s // tm) + j))],
        out_shape=[jax.ShapeDtypeStruct((b, s, d), F32),
                   jax.ShapeDtypeStruct((b, s, d // 2), F32),
                   jax.ShapeDtypeStruct((b, s, d // 2), F32),
                   jax.ShapeDtypeStruct((nblk, PEER_KEYS, b * s), F32)],
        compiler_params=_cparams("parallel", "parallel"),
        name="mixer_output",
    )(x, y_f, y_b, z, y2, zz, x12, fbias.reshape(1, w), g_ssd.reshape(1, w), g_hy.reshape(1, w),
      m_h, w_out, g1, norm2.reshape(1, d), sh2, sc2, wq, sk)


def _topk_rows(xs, riota, k):
    t = xs[0].shape[1]
    slot = lax.broadcasted_iota(jnp.int32, (k, t), 0)
    vals = jnp.zeros((k, t), F32)
    idxs = jnp.zeros((k, t), F32)
    big = jnp.float32(1e9)
    for it in range(k):
        m = xs[0]
        for a in xs[1:]:
            m = jnp.maximum(m, a)
        m = jnp.max(m, axis=0, keepdims=True)
        sel = None
        for a, ri in zip(xs, riota):
            c = jnp.min(jnp.where(a == m, ri, big), axis=0, keepdims=True)
            sel = c if sel is None else jnp.minimum(sel, c)
        vals = jnp.where(slot == it, m, vals)
        idxs = jnp.where(slot == it, sel, idxs)
        xs = [jnp.where(ri == sel, -jnp.inf, a) for a, ri in zip(xs, riota)]
    return vals, idxs


def _topk_kernel(s_ref, idx_ref, gate_ref, idx_t, gate_t):
    k = PEER_TOPK
    t = s_ref.shape[-1]
    key_iota = lax.broadcasted_iota(jnp.int32, (PEER_KEYS, t), 0).astype(F32)
    r16 = lax.broadcasted_iota(jnp.int32, (k, t), 0).astype(F32)
    slot = lax.broadcasted_iota(jnp.int32, (k, t), 0)

    def head(h, carry):
        v1, i1 = _topk_rows([s_ref[2 * h]], [key_iota], k)
        v2, i2 = _topk_rows([s_ref[2 * h + 1]], [key_iota], k)
        cands = [v1[a:a + 1, :] + v2 for a in range(k)]
        ciota = [r16 + float(a * k) for a in range(k)]
        sc, ci = _topk_rows(cands, ciota, k)
        e = jnp.exp(sc - sc[0:1, :])
        ih = jnp.zeros((k, t), F32)
        for j in range(k):
            cj = ci[j:j + 1, :]
            a = jnp.floor(cj * (1.0 / k))
            bb = cj - a * k
            e1 = jnp.sum(jnp.where(r16 == a, i1, 0.0), axis=0, keepdims=True)
            e2 = jnp.sum(jnp.where(r16 == bb, i2, 0.0), axis=0, keepdims=True)
            ih = jnp.where(slot == j, e1 * PEER_KEYS + e2, ih)
        row0 = pl.multiple_of(h * k, k)
        idx_t[pl.ds(row0, k), :] = ih
        gate_t[pl.ds(row0, k), :] = e / jnp.sum(e, axis=0, keepdims=True)
        return carry

    lax.fori_loop(0, PEER_HEADS, head, 0)
    idx_ref[...] = jnp.transpose(idx_t[...]).astype(jnp.int32)
    gate_ref[...] = jnp.transpose(gate_t[...])


def peer_topk(scores_t, tt=128):
    nblk, nk, n = scores_t.shape
    hk = PEER_HEADS * PEER_TOPK
    return pl.pallas_call(
        _topk_kernel,
        grid=(n // tt,),
        in_specs=[pl.BlockSpec((nblk, nk, tt), lambda i: (0, 0, i))],
        out_specs=[pl.BlockSpec((tt, hk), lambda i: (i, 0)), pl.BlockSpec((tt, hk), lambda i: (i, 0))],
        out_shape=[jax.ShapeDtypeStruct((n, hk), jnp.int32), jax.ShapeDtypeStruct((n, hk), F32)],
        scratch_shapes=[pltpu.VMEM((hk, tt), F32), pltpu.VMEM((hk, tt), F32)],
        compiler_params=_cparams("parallel"),
        name="peer_topk",
    )(scores_t)


def _pack_table(tab):
    e, d = tab.shape
    bits = lax.bitcast_convert_type(tab.astype(BF16), jnp.uint16).astype(jnp.uint32)
    return ((bits[:, d // 2:] << 16) | bits[:, :d // 2]).reshape(e, 1, d // 2)


def _unpack(wd):
    lo = pltpu.bitcast(wd << 16, F32)
    hi = pltpu.bitcast(wd & jnp.uint32(0xFFFF0000), F32)
    return lo, hi


def _load_table_once(tab_hbm, tab_vmem, sem):
    @pl.when(pl.program_id(0) == 0)
    def _():
        cp = pltpu.make_async_copy(tab_hbm, tab_vmem, sem)
        cp.start()
        cp.wait()


def _peer_act_kernel(idx_ref, flo_ref, fhi_ref, gate_ref, tab_hbm, act_ref, tab_vmem, sem, *, tok):
    _load_table_once(tab_hbm, tab_vmem, sem)
    hk = gate_ref.shape[-1]
    lane = lax.broadcasted_iota(jnp.int32, (1, hk), 1)

    def token(t, carry):
        flo = flo_ref[t]
        fhi = fhi_ref[t]
        row = jnp.zeros((1, hk), F32)
        for k in range(hk):
            lo, hi = _unpack(tab_vmem[idx_ref[t, k]])
            s = jnp.sum(lo * flo + hi * fhi, axis=1, keepdims=True)
            row = jnp.where(lane == k, s, row)
        act_ref[pl.ds(t, 1), :] = row
        return carry

    lax.fori_loop(0, tok, token, 0)
    pre = act_ref[...]
    act_ref[...] = 0.5 * pre * (1.0 + lax.erf(pre * (1.0 / math.sqrt(2.0)))) * gate_ref[...]


def peer_activations(idx, f_lo, f_hi, gate, table, tok=64):
    n, hk = idx.shape
    half = f_lo.shape[-1]
    kern = functools.partial(_peer_act_kernel, tok=tok)
    return pl.pallas_call(
        kern,
        grid=(n // tok,),
        in_specs=[pl.BlockSpec((tok, hk), lambda i: (i, 0), memory_space=pltpu.SMEM),
                  pl.BlockSpec((tok, 1, half), lambda i: (i, 0, 0)),
                  pl.BlockSpec((tok, 1, half), lambda i: (i, 0, 0)),
                  pl.BlockSpec((tok, hk), lambda i: (i, 0)),
                  pl.BlockSpec(memory_space=pl.ANY)],
        out_specs=pl.BlockSpec((tok, hk), lambda i: (i, 0)),
        out_shape=jax.ShapeDtypeStruct((n, hk), F32),
        scratch_shapes=[pltpu.VMEM(table.shape, table.dtype), pltpu.SemaphoreType.DMA(())],
        compiler_params=_cparams("arbitrary"),
        name="peer_activations",
    )(idx, f_lo, f_hi, gate, table)


def _peer_out_kernel(idx_ref, act_ref, tab_hbm, o_ref, tab_vmem, sem, *, tok):
    _load_table_once(tab_hbm, tab_vmem, sem)
    hk = idx_ref.shape[-1]
    half = tab_vmem.shape[-1]

    def token(t, carry):
        accs = [jnp.zeros((1, half), F32) for _ in range(4)]
        for k in range(hk):
            lo, hi = _unpack(tab_vmem[idx_ref[t, k]])
            a = act_ref[t, k]
            j = 2 * (k % 2)
            accs[j] = accs[j] + a * lo
            accs[j + 1] = accs[j + 1] + a * hi
        o_ref[t] = jnp.concatenate([accs[0] + accs[2], accs[1] + accs[3]], axis=1)
        return carry

    lax.fori_loop(0, tok, token, 0)


def peer_outputs(idx, act, table, tok=64):
    n, hk = idx.shape
    d = 2 * table.shape[-1]
    kern = functools.partial(_peer_out_kernel, tok=tok)
    return pl.pallas_call(
        kern,
        grid=(n // tok,),
        in_specs=[pl.BlockSpec((tok, hk), lambda i: (i, 0), memory_space=pltpu.SMEM),
                  pl.BlockSpec((tok, hk), lambda i: (i, 0), memory_space=pltpu.SMEM),
                  pl.BlockSpec(memory_space=pl.ANY)],
        out_specs=pl.BlockSpec((tok, 1, d), lambda i: (i, 0, 0)),
        out_shape=jax.ShapeDtypeStruct((n, 1, d), F32),
        scratch_shapes=[pltpu.VMEM(table.shape, table.dtype), pltpu.SemaphoreType.DMA(())],
        compiler_params=_cparams("arbitrary"),
        name="peer_outputs",
    )(idx, act, table)


def _final_kernel(h_ref, p_ref, g2_ref, gain_ref, o_ref):
    h = h_ref[0] + g2_ref[0] * p_ref[0]
    o_ref[0] = h * lax.rsqrt(jnp.mean(h * h, axis=-1, keepdims=True) + RMS_EPS) * gain_ref[...]


def final_norm_residual(h1, peer_out, g2, gain, tm=512):
    b, s, d = h1.shape
    row = pl.BlockSpec((1, tm, d), lambda i, j: (i, j, 0))
    return pl.pallas_call(
        _final_kernel,
        grid=(b, s // tm),
        in_specs=[row, row, pl.BlockSpec((1, 1, d), lambda i, j: (i, 0, 0)),
                  pl.BlockSpec((1, d), lambda i, j: (0, 0))],
        out_specs=row,
        out_shape=jax.ShapeDtypeStruct((b, s, d), F32),
        compiler_params=_cparams("parallel", "parallel"),
        name="final_norm",
    )(h1, peer_out, g2, gain.reshape(1, d))


def kernel(x, c, ctx, c_ctx, w_mod, b_mod, norm1, w_in, ssd_conv_w, ssd_conv_b, ssd_a_log, ssd_dt_bias, ssd_d, ssd_norm, hy_short_w, hy_short_b, hy_w1, hy_b1, hy_w2, hy_b2, hy_w3, hy_sin_freq, hy_filt_bias, hy_norm, w_out, norm2, peer_wq, peer_subkeys, peer_u, peer_v, final_norm):
    bsz, seq, d = x.shape
    ctx_len = ctx.shape[1]
    w_ssd = SSD_HEADS * HEAD_DIM
    gn = SSD_GROUPS * SSD_STATE
    conv_dim = w_ssd + 2 * gn
    off_dt = w_ssd + conv_dim
    off_hy = off_dt + 2 * SSD_HEADS
    w_hy = (w_in.shape[-1] - off_hy) // 3
    assert bsz == 2 and seq // GRID_W == 2 * FFT_N2 and w_hy == w_ssd

    i = 0
    c_rows = jnp.zeros((8, d), F32).at[:bsz].set(c).at[bsz].set(c_ctx)
    mod = modulation(c_rows, w_mod[i], b_mod[i])
    mod_l = mod[:bsz].reshape(bsz, 1, 6, d)
    mod_c = jnp.broadcast_to(mod[bsz].reshape(1, 1, 6, d), (bsz, 1, 6, d))
    sh1_l, sc1_l, g1_l, sh2_l, sc2_l, g2_l = [mod_l[:, :, j] for j in range(6)]
    sh1_c, sc1_c = mod_c[:, :, 0], mod_c[:, :, 1]

    wi = w_in[i].astype(BF16)
    wz, wx, wh = wi[:, :w_ssd], wi[:, w_ssd:off_dt], wi[:, off_hy:]
    wd = jnp.zeros((d, LANE), BF16).at[:, :2 * SSD_HEADS].set(wi[:, off_dt:off_hy])
    z_l, xbc_l, dt_l, hy_l = in_projection(x, sh1_l, sc1_l, norm1[i], wz, wx, wd, wh, tm=512)
    _, xbc_c, dt_c, _ = in_projection(ctx, sh1_c, sc1_c, norm1[i], wz, wx, wd, wh, tm=ctx_len)

    xa_l = dwconv(xbc_l, ssd_conv_w[i], ssd_conv_b[i], 0, conv_dim, True, BF16, tm=512)
    xa_c = dwconv(xbc_c, ssd_conv_w[i], ssd_conv_b[i], 0, conv_dim, True, BF16, tm=ctx_len)
    pad_row = lambda v: jnp.zeros((1, LANE), F32).at[0, :v.shape[0]].set(v)
    bias_row = pad_row(ssd_dt_bias[i].reshape(-1))
    a_row = pad_row(-jnp.exp(ssd_a_log[i].reshape(-1)))
    dskip = jnp.repeat(ssd_d[i], HEAD_DIM).reshape(1, w_ssd)
    h0 = jnp.zeros((bsz, SSD_GROUPS, SSD_STATE, SSD_HPG * HEAD_DIM), F32)
    _, s_f = ssd_scan(xa_c, dt_c, bias_row, a_row, dskip, h0, rev=False)
    _, s_b = ssd_scan(xa_c, dt_c, bias_row, a_row, dskip, h0, rev=True)
    y_f, _ = ssd_scan(xa_l, dt_l, bias_row, a_row, dskip, s_f, rev=False)
    y_b, _ = ssd_scan(xa_l, dt_l, bias_row, a_row, dskip, s_b, rev=True)

    v = dwconv(hy_l, hy_short_w[i][:, :w_hy], hy_short_b[i][:w_hy], 0, w_hy, False, F32, tm=512)
    x12 = dwconv(hy_l, hy_short_w[i][:, w_hy:], hy_short_b[i][w_hy:], w_hy, 2 * w_hy, False, BF16, tm=512)
    filt = hyena_filters(seq, hy_w1[i], hy_b1[i], hy_w2[i], hy_b2[i], hy_w3[i], hy_sin_freq[i], w_hy)
    filt = filt.reshape(seq, 2, 2, w_hy)
    h_fwd = jnp.concatenate([filt[:, 0, 0], filt[:, 1, 0]], axis=1)
    h_bwd = jnp.concatenate([filt[:, 0, 1], filt[:, 1, 1]], axis=1)
    kfull = jnp.concatenate([h_fwd, jnp.zeros_like(h_fwd[:1]), h_bwd[:0:-1]], axis=0)
    tab_n = _dft_tables(seq, colmajor=False)
    tab_c = _dft_tables(seq, colmajor=True)
    kspec = fft_inner_spectrum(fft_filter_outer(kfull, tab_n[2]), tab_n[3])
    y1 = long_conv(v, kspec, 0, tab_n, colmajor=False)
    zz = hyena_gate(y1, v, x12, hy_filt_bias[i, 0])
    y2 = long_conv(zz, kspec, w_hy, tab_c, colmajor=True)

    h1, f_lo, f_hi, scores_t = mixer_output(
        x, y_f, y_b, z_l, y2, zz, x12, hy_filt_bias[i, 1], ssd_norm[i], hy_norm[i],
        w_out[i].astype(BF16), g1_l, norm2[i], sh2_l, sc2_l, peer_wq[i].astype(BF16), peer_subkeys[i])

    n_tok = bsz * seq
    idx, gate = peer_topk(scores_t)
    act = peer_activations(idx, f_lo.reshape(n_tok, 1, d // 2), f_hi.reshape(n_tok, 1, d // 2), gate,
                           _pack_table(peer_u[i]))
    p_out = peer_outputs(idx, act, _pack_table(peer_v[i]))
    return final_norm_residual(h1, p_out.reshape(bsz, seq, d), g2_l, final_norm)
```

```python
import functools
import math

import jax
import jax.numpy as jnp
from jax import lax
from jax.experimental import pallas as pl
from jax.experimental.pallas import tpu as pltpu

F32 = jnp.float32
BF16 = jnp.bfloat16
HIGHEST = lax.Precision.HIGHEST

LANE = 128
VMEM_LIMIT = 48 * 1024 * 1024

RMS_EPS = 1e-6
GRID_W = 64
HEAD_DIM = 64
SSD_GROUPS = 4
SSD_HPG = 4
SSD_HEADS = SSD_GROUPS * SSD_HPG
SSD_STATE = 128
SSD_CHUNK = 128
SSD_CONV = 5
HY_SHORT = 3
HY_EMB = 33
HY_BANDS = (HY_EMB - 1) // 2
HY_TARGET = 1e-2
HY_MAX_DECAY = math.log(HY_TARGET) / 0.3
HY_MIN_DECAY = math.log(HY_TARGET) / 1.5
PEER_HEADS = 8
PEER_KEYS = 128
PEER_TOPK = 16
PEER_HALF = 128
FFT_N2 = 128
HALO = 16


def _cparams(*sem):
    return pltpu.CompilerParams(dimension_semantics=sem, vmem_limit_bytes=VMEM_LIMIT)


def _silu(x):
    return x * jax.nn.sigmoid(x)


def _mod_kernel(c_ref, w_ref, b_ref, o_ref):
    s = _silu(c_ref[...])
    o_ref[...] = jnp.dot(s, w_ref[...], precision=HIGHEST, preferred_element_type=F32) + b_ref[...]


def modulation(c_rows, w_mod, b_mod):
    d, n = w_mod.shape
    tn = 512
    return pl.pallas_call(
        _mod_kernel,
        grid=(n // tn,),
        in_specs=[pl.BlockSpec((8, d), lambda j: (0, 0)),
                  pl.BlockSpec((d, tn), lambda j: (0, j)),
                  pl.BlockSpec((1, tn), lambda j: (0, j))],
        out_specs=pl.BlockSpec((8, tn), lambda j: (0, j)),
        out_shape=jax.ShapeDtypeStruct((8, n), F32),
        compiler_params=_cparams("arbitrary"),
        name="modulation",
    )(c_rows, w_mod, b_mod.reshape(1, n))


def _inproj_kernel(x_ref, sh_ref, sc_ref, g_ref, wz_ref, wx_ref, wd_ref, wh_ref,
                   z_ref, xbc_ref, dt_ref, hy_ref):
    x = x_ref[0]
    xn = x * lax.rsqrt(jnp.mean(x * x, axis=-1, keepdims=True) + RMS_EPS) * g_ref[...]
    xm = (xn * (1.0 + sc_ref[0]) + sh_ref[0]).astype(BF16)
    z_ref[0] = jnp.dot(xm, wz_ref[...], preferred_element_type=F32).astype(z_ref.dtype)
    xbc_ref[0] = jnp.dot(xm, wx_ref[...], preferred_element_type=F32).astype(xbc_ref.dtype)
    dt_ref[0] = jnp.dot(xm, wd_ref[...], preferred_element_type=F32)
    hy_ref[0] = jnp.dot(xm, wh_ref[...], preferred_element_type=F32).astype(hy_ref.dtype)


def in_projection(x, shift, scale, gain, wz, wx, wd, wh, tm):
    b, s, d = x.shape
    const = lambda w: pl.BlockSpec(w.shape, lambda i, j: (0, 0), pipeline_mode=pl.Buffered(1))
    row = lambda n: pl.BlockSpec((1, tm, n), lambda i, j: (i, j, 0))
    vec = pl.BlockSpec((1, 1, d), lambda i, j: (i, 0, 0))
    return pl.pallas_call(
        _inproj_kernel,
        grid=(b, s // tm),
        in_specs=[row(d), vec, vec, pl.BlockSpec((1, d), lambda i, j: (0, 0)),
                  const(wz), const(wx), const(wd), const(wh)],
        out_specs=[row(wz.shape[1]), row(wx.shape[1]), row(wd.shape[1]), row(wh.shape[1])],
        out_shape=[jax.ShapeDtypeStruct((b, s, wz.shape[1]), BF16),
                   jax.ShapeDtypeStruct((b, s, wx.shape[1]), BF16),
                   jax.ShapeDtypeStruct((b, s, wd.shape[1]), F32),
                   jax.ShapeDtypeStruct((b, s, wh.shape[1]), BF16)],
        compiler_params=_cparams("parallel", "parallel"),
        name="in_projection",
    )(x, shift, scale, gain.reshape(1, d), wz, wx, wd, wh)


def _dwconv_kernel(prev_ref, main_ref, next_ref, w_ref, b_ref, o_ref, ext_ref, *, taps, act, tm, nt):
    i = pl.program_id(1)
    ext_ref[pl.ds(0, HALO), :] = jnp.where(i > 0, prev_ref[0].astype(F32), 0.0)
    ext_ref[pl.ds(HALO, tm), :] = main_ref[0].astype(F32)
    ext_ref[pl.ds(HALO + tm, HALO), :] = jnp.where(i < nt - 1, next_ref[0].astype(F32), 0.0)
    acc = jnp.broadcast_to(b_ref[...], (tm, b_ref.shape[1]))
    for k in range(taps):
        acc = acc + w_ref[pl.ds(k, 1), :] * ext_ref[pl.ds(HALO - taps // 2 + k, tm), :]
    if act:
        acc = _silu(acc)
    o_ref[0] = acc.astype(o_ref.dtype)


def dwconv(u, w, bias, col0, ncols, act, out_dtype, tm, cb=512):
    b, s, _ = u.shape
    taps = w.shape[0]
    nt = s // tm
    hb = tm // HALO
    c0 = col0 // cb
    w8 = jnp.zeros((8, ncols), F32).at[:taps].set(w)
    kern = functools.partial(_dwconv_kernel, taps=taps, act=act, tm=tm, nt=nt)
    return pl.pallas_call(
        kern,
        grid=(b, nt, ncols // cb),
        in_specs=[pl.BlockSpec((1, HALO, cb), lambda bi, i, j: (bi, jnp.maximum(i * hb - 1, 0), j + c0)),
                  pl.BlockSpec((1, tm, cb), lambda bi, i, j: (bi, i, j + c0)),
                  pl.BlockSpec((1, HALO, cb), lambda bi, i, j: (bi, jnp.minimum((i + 1) * hb, s // HALO - 1), j + c0)),
                  pl.BlockSpec((8, cb), lambda bi, i, j: (0, j)),
                  pl.BlockSpec((1, cb), lambda bi, i, j: (0, j))],
        out_specs=pl.BlockSpec((1, tm, cb), lambda bi, i, j: (bi, i, j)),
        out_shape=jax.ShapeDtypeStruct((b, s, ncols), out_dtype),
        scratch_shapes=[pltpu.VMEM((tm + 2 * HALO, cb), F32)],
        compiler_params=_cparams("parallel", "parallel", "parallel"),
        name="dwconv",
    )(u, u, u, w8, bias.reshape(1, ncols))


def _ssd_kernel(x_ref, b_ref, c_ref, dt_ref, bias_ref, a_ref, dskip_ref, h0_ref,
                y_ref, st_ref, *, rev, col0):
    q = SSD_CHUNK
    gw = SSD_HPG * HEAD_DIM

    @pl.when(pl.program_id(1) == 0)
    def _():
        st_ref[...] = h0_ref[...]

    dtp = jax.nn.softplus(dt_ref[0] + bias_ref[...])
    a = dtp * a_ref[...]
    li = lax.broadcasted_iota(jnp.int32, (q, q), 0)
    si = lax.broadcasted_iota(jnp.int32, (q, q), 1)
    tri = (si <= li).astype(F32)
    cum = jnp.dot(tri, a, precision=HIGHEST, preferred_element_type=F32)
    total = cum[q - 1:q, :]
    if rev:
        r = a - cum
        keep = si >= li
        d_out = jnp.exp(total + r)
        d_st = jnp.exp(-r)
    else:
        r = cum
        keep = li >= si
        d_out = jnp.exp(r)
        d_st = jnp.exp(total - r)
    r_t = jnp.transpose(r)
    d_tot = jnp.exp(total)
    lane_head = lax.broadcasted_iota(jnp.int32, (1, gw), 1) // HEAD_DIM

    x = x_ref[0].astype(F32)
    for g in range(SSD_GROUPS):
        cg = c_ref[0, :, g * SSD_STATE:(g + 1) * SSD_STATE]
        bg = b_ref[0, :, g * SSD_STATE:(g + 1) * SSD_STATE]
        cb = lax.dot_general(cg, bg, (((1,), (1,)), ((), ())), preferred_element_type=F32)
        xg = x[:, g * gw:(g + 1) * gw]
        dt_g = jnp.zeros((q, gw), F32)
        dout_g = jnp.zeros((q, gw), F32)
        dst_g = jnp.zeros((q, gw), F32)
        dtot_g = jnp.zeros((1, gw), F32)
        for hh in range(SSD_HPG):
            hc = col0 + g * SSD_HPG + hh
            m = lane_head == hh
            dt_g = jnp.where(m, dtp[:, hc:hc + 1], dt_g)
            dout_g = jnp.where(m, d_out[:, hc:hc + 1], dout_g)
            dst_g = jnp.where(m, d_st[:, hc:hc + 1], dst_g)
            dtot_g = jnp.where(m, d_tot[:, hc:hc + 1], dtot_g)
        xdt = xg * dt_g
        st_old = st_ref[0, g]
        y_g = jnp.dot(cg, st_old.astype(BF16), preferred_element_type=F32) * dout_g
        for hh in range(SSD_HPG):
            hc = col0 + g * SSD_HPG + hh
            diff = r[:, hc:hc + 1] - r_t[hc:hc + 1, :]
            lm = jnp.where(keep, jnp.exp(jnp.minimum(diff, 0.0)), 0.0)
            mh = (cb * lm).astype(BF16)
            xh = jnp.where(lane_head == hh, xdt, 0.0).astype(BF16)
            y_g = y_g + jnp.dot(mh, xh, preferred_element_type=F32)
        if not rev:
            y_g = y_g + xg * dskip_ref[:, g * gw:(g + 1) * gw]
        y_ref[0, :, g * gw:(g + 1) * gw] = y_g.astype(y_ref.dtype)
        upd = lax.dot_general(bg, (xdt * dst_g).astype(BF16), (((0,), (0,)), ((), ())),
                              preferred_element_type=F32)
        st_ref[0, g] = st_old * dtot_g + upd


def ssd_scan(xbc, dt_raw, dt_bias_row, a_row, dskip_row, h0, rev):
    b, s, _ = xbc.shape
    w = SSD_HEADS * HEAD_DIM
    gn = SSD_GROUPS * SSD_STATE
    nc = s // SSD_CHUNK
    cidx = (lambda c: nc - 1 - c) if rev else (lambda c: c)
    kern = functools.partial(_ssd_kernel, rev=rev, col0=SSD_HEADS if rev else 0)
    st_spec = pl.BlockSpec((1, SSD_GROUPS, SSD_STATE, SSD_HPG * HEAD_DIM), lambda bi, c: (bi, 0, 0, 0))
    return pl.pallas_call(
        kern,
        grid=(b, nc),
        in_specs=[pl.BlockSpec((1, SSD_CHUNK, w), lambda bi, c: (bi, cidx(c), 0)),
                  pl.BlockSpec((1, SSD_CHUNK, gn), lambda bi, c: (bi, cidx(c), w // gn)),
                  pl.BlockSpec((1, SSD_CHUNK, gn), lambda bi, c: (bi, cidx(c), w // gn + 1)),
                  pl.BlockSpec((1, SSD_CHUNK, LANE), lambda bi, c: (bi, cidx(c), 0)),
                  pl.BlockSpec((1, LANE), lambda bi, c: (0, 0)),
                  pl.BlockSpec((1, LANE), lambda bi, c: (0, 0)),
                  pl.BlockSpec((1, w), lambda bi, c: (0, 0)),
                  st_spec],
        out_specs=[pl.BlockSpec((1, SSD_CHUNK, w), lambda bi, c: (bi, cidx(c), 0)), st_spec],
        out_shape=[jax.ShapeDtypeStruct((b, s, w), BF16),
                   jax.ShapeDtypeStruct(h0.shape, F32)],
        compiler_params=_cparams("parallel", "arbitrary"),
        name="ssd_scan_rev" if rev else "ssd_scan_fwd",
    )(xbc, xbc, xbc, dt_raw, dt_bias_row, a_row, dskip_row, h0)


def _filter_kernel(z_ref, w1_ref, b1_ref, w2_ref, b2_ref, w3_ref, f0_ref, f1_ref, dl_ref, o_ref, *, seqlen):
    z = z_ref[...]
    h = jnp.sin(f0_ref[...] * (jnp.dot(z, w1_ref[...], precision=HIGHEST, preferred_element_type=F32) + b1_ref[...]))
    h = jnp.sin(f1_ref[...] * (jnp.dot(h, w2_ref[...], precision=HIGHEST, preferred_element_type=F32) + b2_ref[...]))
    h = jnp.dot(h, w3_ref[...], precision=HIGHEST, preferred_element_type=F32)
    tl = z.shape[0]
    n = pl.program_id(0) * tl + lax.broadcasted_iota(jnp.int32, (tl, 1), 0)
    o_ref[...] = jnp.where(n == seqlen, 0.0, h * jnp.exp(-z[:, 0:1] * dl_ref[...]))


def hyena_filters(seqlen, w1, b1, w2, b2, w3, sin_freq, width):
    t = jnp.linspace(0.0, 1.0, seqlen, dtype=F32)[:, None]
    w_ang = 2.0 * math.pi * jnp.arange(seqlen, dtype=F32) / seqlen
    bands = jnp.linspace(1e-4, HY_BANDS - 1, HY_BANDS, dtype=F32)
    ang = w_ang[:, None] * bands[None, :]
    zpos = jnp.concatenate([t, jnp.cos(ang), -jnp.sin(ang)], axis=-1)
    zboth = jnp.concatenate([zpos, jnp.zeros_like(zpos[:1]), zpos[:0:-1]], axis=0)
    zpad = jnp.zeros((2 * seqlen, LANE), F32).at[:, :HY_EMB].set(zboth)
    hid = w1.shape[1]
    w1p = jnp.zeros((LANE, hid), F32).at[:HY_EMB].set(w1)
    deltas = jnp.abs(jnp.linspace(HY_MIN_DECAY, HY_MAX_DECAY, width, dtype=F32))
    dl = jnp.tile(deltas, 2).reshape(1, 2 * width)
    tl = 512
    nfwd = seqlen // tl
    full = lambda a: pl.BlockSpec(a.shape, lambda i, j: (0, 0))
    return pl.pallas_call(
        functools.partial(_filter_kernel, seqlen=seqlen),
        grid=(2 * seqlen // tl, 2),
        in_specs=[pl.BlockSpec((tl, LANE), lambda i, j: (i, 0)),
                  full(w1p), pl.BlockSpec((1, hid), lambda i, j: (0, 0)),
                  full(w2), pl.BlockSpec((1, hid), lambda i, j: (0, 0)),
                  pl.BlockSpec((hid, width), lambda i, j: (0, 2 * j + i // nfwd)),
                  pl.BlockSpec((1, hid), lambda i, j: (0, 0)),
                  pl.BlockSpec((1, hid), lambda i, j: (0, 0)),
                  pl.BlockSpec((1, width), lambda i, j: (0, j))],
        out_specs=pl.BlockSpec((tl, width), lambda i, j: (i, j)),
        out_shape=jax.ShapeDtypeStruct((2 * seqlen, 2 * width), F32),
        compiler_params=_cparams("parallel", "parallel"),
        name="hyena_filters",
    )(zpad, w1p, b1.reshape(1, hid), w2, b2.reshape(1, hid), w3,
      sin_freq[0].reshape(1, hid), sin_freq[1].reshape(1, hid), dl)


def _dft_tables(seqlen, colmajor):
    n = 2 * seqlen
    n2 = FFT_N2
    n1 = n // n2
    h1 = n1 // 2
    k1 = jnp.arange(n1, dtype=jnp.int32)
    m1 = jnp.arange(h1, dtype=jnp.int32)
    if colmajor:
        m1 = 2 * (m1 % GRID_W) + m1 // GRID_W
    ph = (k1[:, None] * m1[None, :] * n2) % n
    ang = (-2.0 * math.pi / n) * ph.astype(F32)
    wr, wi = jnp.cos(ang), jnp.sin(ang)
    w_fwd = jnp.concatenate([jnp.concatenate([wr, -wi], 1), jnp.concatenate([wi, wr], 1)], 0)
    w_inv = jnp.concatenate([jnp.concatenate([wr.T, wi.T], 1), jnp.concatenate([-wi.T, wr.T], 1)], 0) / n
    mf = jnp.arange(n1, dtype=jnp.int32)
    phf = (k1[:, None] * mf[None, :] * n2) % n
    angf = (-2.0 * math.pi / n) * phf.astype(F32)
    w_flt = jnp.concatenate([jnp.cos(angf), jnp.sin(angf)], 0)
    a2 = jnp.arange(n2, dtype=jnp.int32)
    ph2 = (a2[None, :, None] * a2[None, None, :] * n1 + k1[:, None, None] * a2[None, None, :]) % n
    ang2 = (-2.0 * math.pi / n) * ph2.astype(F32)
    c2, s2 = jnp.cos(ang2), jnp.sin(ang2)
    w2_fwd = jnp.concatenate([jnp.concatenate([c2, -s2], 2), jnp.concatenate([s2, c2], 2)], 1)
    c2t, s2t = jnp.swapaxes(c2, 1, 2), jnp.swapaxes(s2, 1, 2)
    w2_inv = jnp.concatenate([jnp.concatenate([c2t, s2t], 2), jnp.concatenate([-s2t, c2t], 2)], 1)
    return (w_fwd.astype(BF16), w_inv.astype(BF16), w_flt.astype(BF16),
            w2_fwd.astype(BF16), w2_inv.astype(BF16))


def _fft_outer_fwd_kernel(x_ref, w_ref, o_ref, *, colmajor, nb):
    n1 = o_ref.shape[0]
    for j in range(8):
        parts = []
        for bi in range(nb):
            if colmajor:
                parts.append(x_ref[bi, 0, pl.ds(j * GRID_W, GRID_W), :].astype(F32))
                parts.append(x_ref[bi, 1, pl.ds(j * GRID_W, GRID_W), :].astype(F32))
            else:
                parts.append(x_ref[bi, :, j, :])
        xs = jnp.concatenate(parts, axis=0).astype(BF16)
        a = jnp.dot(w_ref[...], xs, preferred_element_type=F32)
        o_ref[:, 0, j, :] = a[:n1]
        o_ref[:, 1, j, :] = a[n1:]


def fft_outer_fwd(x, w, colmajor, cblk=512):
    b, seqlen, c = x.shape
    n2 = FFT_N2
    n1 = 2 * seqlen // n2
    if colmajor:
        xv = x.reshape(b, 2, seqlen // 2, c)
        in_spec = pl.BlockSpec((b, 2, 8 * GRID_W, cblk), lambda g, j: (0, 0, g, j))
    else:
        xv = x.reshape(b, n1 // 2, n2, c)
        in_spec = pl.BlockSpec((b, n1 // 2, 8, cblk), lambda g, j: (0, 0, g, j))
    kern = functools.partial(_fft_outer_fwd_kernel, colmajor=colmajor, nb=b)
    return pl.pallas_call(
        kern,
        grid=(n2 // 8, c // cblk),
        in_specs=[in_spec, pl.BlockSpec(w.shape, lambda g, j: (0, 0))],
        out_specs=pl.BlockSpec((n1, 2, 8, cblk), lambda g, j: (0, 0, g, j)),
        out_shape=jax.ShapeDtypeStruct((n1, 2, n2, c), F32),
        compiler_params=_cparams("parallel", "parallel"),
        name="fft_outer_fwd_cm" if colmajor else "fft_outer_fwd",
    )(xv, w)


def _fft_filter_outer_kernel(k_ref, w_ref, o_ref):
    n1 = o_ref.shape[0]
    for j in range(8):
        a = jnp.dot(w_ref[...], k_ref[:, j, :].astype(BF16), preferred_element_type=F32)
        o_ref[:, 0, j, :] = a[:n1]
        o_ref[:, 1, j, :] = a[n1:]


def fft_filter_outer(k, w, cblk=512):
    n, c = k.shape
    n2 = FFT_N2
    n1 = n // n2
    return pl.pallas_call(
        _fft_filter_outer_kernel,
        grid=(n2 // 8, c // cblk),
        in_specs=[pl.BlockSpec((n1, 8, cblk), lambda g, j: (0, g, j)),
                  pl.BlockSpec(w.shape, lambda g, j: (0, 0))],
        out_specs=pl.BlockSpec((n1, 2, 8, cblk), lambda g, j: (0, 0, g, j)),
        out_shape=jax.ShapeDtypeStruct((n1, 2, n2, c), F32),
        compiler_params=_cparams("parallel", "parallel"),
        name="fft_filter_outer",
    )(k.reshape(n1, n2, c), w)


def _fft_inner_spec_kernel(a_ref, wf_ref, o_ref):
    n2 = FFT_N2
    cb = a_ref.shape[-1]
    for j in range(8):
        xs = a_ref[j].reshape(2 * n2, cb).astype(BF16)
        o_ref[j] = jnp.dot(wf_ref[j], xs, preferred_element_type=F32).reshape(2, n2, cb)


def fft_inner_spectrum(a, w2_fwd, cblk=512):
    n1, _, n2, c = a.shape
    blk = pl.BlockSpec((8, 2, n2, cblk), lambda g, j: (g, 0, 0, j))
    return pl.pallas_call(
        _fft_inner_spec_kernel,
        grid=(n1 // 8, c // cblk),
        in_specs=[blk, pl.BlockSpec((8, 2 * n2, 2 * n2), lambda g, j: (g, 0, 0))],
        out_specs=blk,
        out_shape=jax.ShapeDtypeStruct(a.shape, F32),
        compiler_params=_cparams("parallel", "arbitrary"),
        name="fft_inner_spectrum",
    )(a, w2_fwd)


def _fft_inner_conv_kernel(a_ref, k_ref, wf_ref, wi_ref, o_ref):
    n2 = FFT_N2
    cb = a_ref.shape[-1]
    for j in range(8):
        xs = a_ref[j].reshape(2 * n2, cb).astype(BF16)
        s = jnp.dot(wf_ref[j], xs, preferred_element_type=F32)
        sr, si = s[:n2], s[n2:]
        kr, ki = k_ref[j, 0], k_ref[j, 1]
        y = jnp.concatenate([sr * kr - si * ki, sr * ki + si * kr], axis=0).astype(BF16)
        z = jnp.dot(wi_ref[j], y, preferred_element_type=F32)
        o_ref[:, 0, j, :] = z[:n2]
        o_ref[:, 1, j, :] = z[n2:]


def fft_inner_conv(a, kspec, kcol0, w2_fwd, w2_inv, cblk=512):
    n1, _, n2, c = a.shape
    kc = kcol0 // cblk
    wspec = pl.BlockSpec((8, 2 * n2, 2 * n2), lambda g, j: (g, 0, 0))
    return pl.pallas_call(
        _fft_inner_conv_kernel,
        grid=(n1 // 8, c // cblk),
        in_specs=[pl.BlockSpec((8, 2, n2, cblk), lambda g, j: (g, 0, 0, j)),
                  pl.BlockSpec((8, 2, n2, cblk), lambda g, j: (g, 0, 0, j + kc)),
                  wspec, wspec],
        out_specs=pl.BlockSpec((n2, 2, 8, cblk), lambda g, j: (0, 0, g, j)),
        out_shape=jax.ShapeDtypeStruct((n2, 2, n1, c), F32),
        compiler_params=_cparams("parallel", "arbitrary"),
        name="fft_inner_conv",
    )(a, kspec, w2_fwd, w2_inv)


def _fft_outer_inv_kernel(b_ref, w_ref, o_ref, *, colmajor, nb):
    cb = b_ref.shape[-1]
    n1 = b_ref.shape[2]
    h1 = n1 // 2
    for j in range(8):
        xs = b_ref[j].reshape(2 * n1, cb).astype(BF16)
        y = jnp.dot(w_ref[...], xs, preferred_element_type=F32)
        for bi in range(nb):
            yb = y[bi * h1:(bi + 1) * h1]
            if colmajor:
                o_ref[bi, 0, pl.ds(j * GRID_W, GRID_W), :] = yb[:GRID_W]
                o_ref[bi, 1, pl.ds(j * GRID_W, GRID_W), :] = yb[GRID_W:]
            else:
                o_ref[bi, :, j, :] = yb


def fft_outer_inv(bsp, w, nb, colmajor, cblk=512):
    n2, _, n1, c = bsp.shape
    seqlen = n1 * n2 // 2
    if colmajor:
        out_spec = pl.BlockSpec((nb, 2, 8 * GRID_W, cblk), lambda g, j: (0, 0, g, j))
        out_shape = jax.ShapeDtypeStruct((nb, 2, seqlen // 2, c), F32)
    else:
        out_spec = pl.BlockSpec((nb, n1 // 2, 8, cblk), lambda g, j: (0, 0, g, j))
        out_shape = jax.ShapeDtypeStruct((nb, n1 // 2, n2, c), F32)
    kern = functools.partial(_fft_outer_inv_kernel, colmajor=colmajor, nb=nb)
    y = pl.pallas_call(
        kern,
        grid=(n2 // 8, c // cblk),
        in_specs=[pl.BlockSpec((8, 2, n1, cblk), lambda g, j: (g, 0, 0, j)),
                  pl.BlockSpec(w.shape, lambda g, j: (0, 0))],
        out_specs=out_spec,
        out_shape=out_shape,
        compiler_params=_cparams("parallel", "parallel"),
        name="fft_outer_inv_cm" if colmajor else "fft_outer_inv",
    )(bsp, w)
    return y.reshape(nb, seqlen, c)


def long_conv(u, kspec, kcol0, tables, colmajor):
    w_fwd, w_inv, _, w2_fwd, w2_inv = tables
    a = fft_outer_fwd(u, w_fwd, colmajor)
    bsp = fft_inner_conv(a, kspec, kcol0, w2_fwd, w2_inv)
    return fft_outer_inv(bsp, w_inv, u.shape[0], colmajor)


def _hy_gate_kernel(y_ref, v_ref, x1_ref, fb_ref, o_ref):
    o_ref[0] = x1_ref[0].astype(F32) * (y_ref[0] + v_ref[0] * fb_ref[...])


def hyena_gate(y, v, x12, fbias, tm=512, cb=512):
    b, s, c = y.shape
    blk = lambda: pl.BlockSpec((1, tm, cb), lambda bi, i, j: (bi, i, j))
    return pl.pallas_call(
        _hy_gate_kernel,
        grid=(b, s // tm, c // cb),
        in_specs=[blk(), blk(), blk(), pl.BlockSpec((1, cb), lambda bi, i, j: (0, j))],
        out_specs=blk(),
        out_shape=jax.ShapeDtypeStruct((b, s, c), F32),
        compiler_params=_cparams("parallel", "parallel", "parallel"),
        name="hyena_gate",
    )(y, v, x12, fbias.reshape(1, c))


def _mix_kernel(x_ref, yf_ref, yb_ref, z_ref, y2_ref, zz_ref, x12_ref,
                fb_ref, gs_ref, gh_ref, mh_ref, wo_ref, g1_ref, n2_ref, sh2_ref, sc2_ref,
                wq_ref, sk_ref, h1_ref, f_ref, st_ref):
    w = yf_ref.shape[-1]
    gw = w // SSD_GROUPS
    ys = (yf_ref[0].astype(F32) + yb_ref[0].astype(F32)) * _silu(z_ref[0].astype(F32))
    parts = []
    for g in range(SSD_GROUPS):
        blk = ys[:, g * gw:(g + 1) * gw]
        ms = jnp.mean(blk * blk, axis=-1, keepdims=True)
        parts.append(blk * lax.rsqrt(ms + RMS_EPS))
    o_s = jnp.concatenate(parts, axis=1) * gs_ref[...]
    yh = x12_ref[0].astype(F32) * (y2_ref[0] + zz_ref[0] * fb_ref[...])
    parts = []
    for c in range(w // LANE):
        blk = yh[:, c * LANE:(c + 1) * LANE]
        sq = blk * blk
        hi = sq.astype(BF16)
        lo = (sq - hi.astype(F32)).astype(BF16)
        ms = (jnp.dot(hi, mh_ref[...], preferred_element_type=F32)
              + jnp.dot(lo, mh_ref[...], preferred_element_type=F32)) * (1.0 / HEAD_DIM)
        parts.append(blk * lax.rsqrt(ms + RMS_EPS))
    o_h = jnp.concatenate(parts, axis=1) * gh_ref[...]
    mix = (jnp.dot(o_s.astype(BF16), wo_ref[pl.ds(0, w), :], preferred_element_type=F32)
           + jnp.dot(o_h.astype(BF16), wo_ref[pl.ds(w, w), :], preferred_element_type=F32))
    h1 = x_ref[0] + g1_ref[0] * mix
    h1_ref[0] = h1
    hn = h1 * lax.rsqrt(jnp.mean(h1 * h1, axis=-1, keepdims=True) + RMS_EPS) * n2_ref[...]
    f = hn * (1.0 + sc2_ref[0]) + sh2_ref[0]
    f_ref[0] = f
    qv =jnp.dot(f.astype(BF16), wq_ref[...], preferred_element_type=F32).astype(BF16)
    for blk in range(2 * PEER_HEADS):
        qb = qv[:, blk * PEER_HALF:(blk + 1) * PEER_HALF]
        st_ref[blk] = lax.dot_general(sk_ref[blk % 2], qb, (((1,), (1,)), ((), ())),
                                      preferred_element_type=F32)


def mixer_output(x, y_f, y_b, z, y2, zz, x12, fbias, g_ssd, g_hy, w_out, g1, norm2, sh2, sc2, wq, subkeys, tm=256):
    b, s, d = x.shape
    w = y_f.shape[-1]
    nblk = 2 * PEER_HEADS
    lane = jnp.arange(LANE)
    m_h = (lane[:, None] // HEAD_DIM == lane[None, :] // HEAD_DIM).astype(BF16)
    row = lambda n, c0=0: pl.BlockSpec((1, tm, n), lambda i, j: (i, j, c0))
    vec = lambda n: pl.BlockSpec((1, 1, n), lambda i, j: (i, 0, 0))
    const = lambda a: pl.BlockSpec(a.shape, lambda i, j: (0,) * a.ndim, pipeline_mode=pl.Buffered(1))
    sk = subkeys.astype(BF16)
    return pl.pallas_call(
        _mix_kernel,
        grid=(b, s // tm),
        in_specs=[row(d), row(w), row(w), row(w), row(w), row(w), row(w, 1),
                  pl.BlockSpec((1, w), lambda i, j: (0, 0)),
                  pl.BlockSpec((1, w), lambda i, j: (0, 0)),
                  pl.BlockSpec((1, w), lambda i, j: (0, 0)),
                  const(m_h), const(w_out), vec(d),
                  pl.BlockSpec((1, d), lambda i, j: (0, 0)), vec(d), vec(d),
                  const(wq), const(sk)],
        out_specs=[row(d), row(d),
                   pl.BlockSpec((nblk, PEER_KEYS, tm), lambda i, j: (0, 0, i * (s // tm) + j))],
        out_shape=[jax.ShapeDtypeStruct((b, s, d), F32),
                   jax.ShapeDtypeStruct((b, s, d), F32),
                   jax.ShapeDtypeStruct((nblk, PEER_KEYS, b * s), F32)],
        compiler_params=_cparams("parallel", "parallel"),
        name="mixer_output",
    )(x, y_f, y_b, z, y2, zz, x12, fbias.reshape(1, w), g_ssd.reshape(1, w), g_hy.reshape(1, w),
      m_h, w_out, g1, norm2.reshape(1, d), sh2, sc2, wq, sk)


def _topk_rows(xs, riota, k):
    t = xs[0].shape[1]
    slot = lax.broadcasted_iota(jnp.int32, (k, t), 0)
    vals = jnp.zeros((k, t), F32)
    idxs = jnp.zeros((k, t), F32)
    big = jnp.float32(1e9)
    for it in range(k):
        m = xs[0]
        for a in xs[1:]:
            m = jnp.maximum(m, a)
        m = jnp.max(m, axis=0, keepdims=True)
        sel = None
        for a, ri in zip(xs, riota):
            c = jnp.min(jnp.where(a == m, ri, big), axis=0, keepdims=True)
            sel = c if sel is None else jnp.minimum(sel, c)
        vals = jnp.where(slot == it, m, vals)
        idxs = jnp.where(slot == it, sel, idxs)
        xs = [jnp.where(ri == sel, -jnp.inf, a) for a, ri in zip(xs, riota)]
    return vals, idxs


def _topk_kernel(s_ref, idx_ref, gate_ref, idx_t, gate_t):
    k = PEER_TOPK
    t = s_ref.shape[-1]
    key_iota = lax.broadcasted_iota(jnp.int32, (PEER_KEYS, t), 0).astype(F32)
    r16 = lax.broadcasted_iota(jnp.int32, (k, t), 0).astype(F32)
    slot = lax.broadcasted_iota(jnp.int32, (k, t), 0)

    def head(h, carry):
        v1, i1 = _topk_rows([s_ref[2 * h]], [key_iota], k)
        v2, i2 = _topk_rows([s_ref[2 * h + 1]], [key_iota], k)
        cands = [v1[a:a + 1, :] + v2 for a in range(k)]
        ciota = [r16 + float(a * k) for a in range(k)]
        sc, ci = _topk_rows(cands, ciota, k)
        e = jnp.exp(sc - sc[0:1, :])
        ih = jnp.zeros((k, t), F32)
        for j in range(k):
            cj = ci[j:j + 1, :]
            a = jnp.floor(cj * (1.0 / k))
            bb = cj - a * k
            e1 = jnp.sum(jnp.where(r16 == a, i1, 0.0), axis=0, keepdims=True)
            e2 = jnp.sum(jnp.where(r16 == bb, i2, 0.0), axis=0, keepdims=True)
            ih = jnp.where(slot == j, e1 * PEER_KEYS + e2, ih)
        row0 = pl.multiple_of(h * k, k)
        idx_t[pl.ds(row0, k), :] = ih
        gate_t[pl.ds(row0, k), :] = e / jnp.sum(e, axis=0, keepdims=True)
        return carry

    lax.fori_loop(0, PEER_HEADS, head, 0)
    idx_ref[...] = jnp.transpose(idx_t[...]).astype(jnp.int32)
    gate_ref[...] = jnp.transpose(gate_t[...])


def peer_topk(scores_t, tt=128):
    nblk, nk, n = scores_t.shape
    hk = PEER_HEADS * PEER_TOPK
    return pl.pallas_call(
        _topk_kernel,
        grid=(n // tt,),
        in_specs=[pl.BlockSpec((nblk, nk, tt), lambda i: (0, 0, i))],
        out_specs=[pl.BlockSpec((tt, hk), lambda i: (i, 0)), pl.BlockSpec((tt, hk), lambda i: (i, 0))],
        out_shape=[jax.ShapeDtypeStruct((n, hk), jnp.int32), jax.ShapeDtypeStruct((n, hk), F32)],
        scratch_shapes=[pltpu.VMEM((hk, tt), F32), pltpu.VMEM((hk, tt), F32)],
        compiler_params=_cparams("parallel"),
        name="peer_topk",
    )(scores_t)


HALF_SUB = 4


def _pack_table(tab):
    e, d = tab.shape
    bits = lax.bitcast_convert_type(tab.astype(BF16), jnp.uint16).astype(jnp.uint32)
    return ((bits[:, d // 2:] << 16) | bits[:, :d // 2]).reshape(e, d // (2 * LANE), LANE)


def _pair_rows(tab_vmem, idx_ref, t, ka, kb):
    row = idx_ref.at[t]
    return jnp.concatenate([tab_vmem[row[ka]], tab_vmem[row[kb]]], axis=0)


def _unpack(wd):
    lo = pltpu.bitcast(wd << 16, F32)
    hi = pltpu.bitcast(wd & jnp.uint32(0xFFFF0000), F32)
    return lo, hi


def _load_table_once(tab_hbm, tab_vmem, sem):
    @pl.when(pl.program_id(0) == 0)
    def _():
        cp = pltpu.make_async_copy(tab_hbm, tab_vmem, sem)
        cp.start()
        cp.wait()


def _peer_act_kernel(idx_ref, f_ref, gate_ref, tab_hbm, act_ref, tab_vmem, part_ref, sem, *, tok):
    _load_table_once(tab_hbm, tab_vmem, sem)
    hk = gate_ref.shape[-1]
    sub = lax.broadcasted_iota(jnp.int32, (8, LANE), 0)
    low = sub < HALF_SUB
    eye = (lax.broadcasted_iota(jnp.int32, (hk, LANE), 0)
           == lax.broadcasted_iota(jnp.int32, (hk, LANE), 1))

    def split_f(t):
        f8 = f_ref[t]
        fsw = pltpu.roll(f8, HALF_SUB, axis=0)
        return jnp.where(low, f8, fsw), jnp.where(low, fsw, f8)

    def partial_sums(i, carry):
        toks = (2 * i, 2 * i + 1)
        fs = [split_f(t) for t in toks]
        for g in range(hk // 8):
            for t, (flo, fhi) in zip(toks, fs):
                merged = None
                for j in range(HALF_SUB):
                    lo, hi = _unpack(_pair_rows(tab_vmem, idx_ref, t, 8 * g + j, 8 * g + HALF_SUB + j))
                    p = lo * flo + hi * fhi
                    p = p + pltpu.roll(p, 6, axis=0)
                    p = p + pltpu.roll(p, 7, axis=0)
                    if j:
                        p = pltpu.roll(p, j, axis=0)
                    merged = p if merged is None else jnp.where((sub & (HALF_SUB - 1)) == j, p, merged)
                part_ref[t, pl.ds(8 * g, 8), :] = merged
        return carry

    def lane_sums(i, carry):
        rows = []
        for s8 in range(8):
            s = jnp.sum(part_ref[8 * i + s8], axis=1, keepdims=True)
            rows.append(jnp.sum(jnp.where(eye, s, 0.0), axis=0, keepdims=True))
        act_ref[pl.ds(pl.multiple_of(8 * i, 8), 8), :] = jnp.concatenate(rows, axis=0)
        return carry

    lax.fori_loop(0, tok // 2, partial_sums, 0)
    lax.fori_loop(0, tok // 8, lane_sums, 0)
    pre = act_ref[...]
    act_ref[...] = 0.5 * pre * (1.0 + lax.erf(pre * (1.0 / math.sqrt(2.0)))) * gate_ref[...]


def peer_activations(idx, f8, gate, table, tok=64):
    n, hk = idx.shape
    kern = functools.partial(_peer_act_kernel, tok=tok)
    return pl.pallas_call(
        kern,
        grid=(n // tok,),
        in_specs=[pl.BlockSpec((tok, hk), lambda i: (i, 0), memory_space=pltpu.SMEM),
                  pl.BlockSpec((tok, 8, LANE), lambda i: (i, 0, 0)),
                  pl.BlockSpec((tok, hk), lambda i: (i, 0)),
                  pl.BlockSpec(memory_space=pl.ANY)],
        out_specs=pl.BlockSpec((tok, hk), lambda i: (i, 0)),
        out_shape=jax.ShapeDtypeStruct((n, hk), F32),
        scratch_shapes=[pltpu.VMEM(table.shape, table.dtype), pltpu.VMEM((tok, hk, LANE), F32),
                        pltpu.SemaphoreType.DMA(())],
        compiler_params=_cparams("arbitrary"),
        name="peer_activations",
    )(idx, f8, gate, table)


def _peer_out_kernel(idx_ref, act_ref, tab_hbm, o_ref, tab_vmem, splat_a, splat_b, sem, *, tok):
    _load_table_once(tab_hbm, tab_vmem, sem)
    hk = idx_ref.shape[-1]
    low = lax.broadcasted_iota(jnp.int32, (8, LANE), 0) < HALF_SUB
    eye = (lax.broadcasted_iota(jnp.int32, (hk, LANE), 0)
           == lax.broadcasted_iota(jnp.int32, (hk, LANE), 1))
    ones = jnp.ones((LANE, LANE), BF16)

    def make_splat(ref, t):
        for s in range(2):
            diag = jnp.where(eye, act_ref[pl.ds(t + s, 1), :], 0.0)
            d_hi = diag.astype(BF16)
            d_lo = (diag - d_hi.astype(F32)).astype(BF16)
            ref[s] = (jnp.dot(d_hi, ones, preferred_element_type=F32)
                      + jnp.dot(d_lo, ones, preferred_element_type=F32))

    def accumulate(ref, t0):
        for s in range(2):
            t = t0 + s
            acc_lo = [jnp.zeros((8, LANE), F32) for _ in range(2)]
            acc_hi = [jnp.zeros((8, LANE), F32) for _ in range(2)]
            for j in range(hk // 2):
                lo, hi = _unpack(_pair_rows(tab_vmem, idx_ref, t, 2 * j, 2 * j + 1))
                a8 = jnp.where(low, ref[s, pl.ds(2 * j, 1), :], ref[s, pl.ds(2 * j + 1, 1), :])
                acc_lo[j % 2] = acc_lo[j % 2] + a8 * lo
                acc_hi[j % 2] = acc_hi[j % 2] + a8 * hi
            lo8 = acc_lo[0] + acc_lo[1]
            hi8 = acc_hi[0] + acc_hi[1]
            lo8 = lo8 + pltpu.roll(lo8, HALF_SUB, axis=0)
            hi8 = hi8 + pltpu.roll(hi8, HALF_SUB, axis=0)
            o_ref[t] = jnp.where(low, lo8, hi8)

    def body(i, carry):
        t0 = 4 * i
        make_splat(splat_b, t0 + 2)
        accumulate(splat_a, t0)
        make_splat(splat_a, jnp.minimum(t0 + 4, tok - 2))
        accumulate(splat_b, t0 + 2)
        return carry

    make_splat(splat_a, 0)
    lax.fori_loop(0, tok // 4, body, 0)


def peer_outputs(idx, act, table, tok=64):
    n, hk = idx.shape
    kern = functools.partial(_peer_out_kernel, tok=tok)
    return pl.pallas_call(
        kern,
        grid=(n // tok,),
        in_specs=[pl.BlockSpec((tok, hk), lambda i: (i, 0), memory_space=pltpu.SMEM),
                  pl.BlockSpec((tok, hk), lambda i: (i, 0)),
                  pl.BlockSpec(memory_space=pl.ANY)],
        out_specs=pl.BlockSpec((tok, 8, LANE), lambda i: (i, 0, 0)),
        out_shape=jax.ShapeDtypeStruct((n, 8, LANE), F32),
        scratch_shapes=[pltpu.VMEM(table.shape, table.dtype), pltpu.VMEM((2, hk, LANE), F32),
                        pltpu.VMEM((2, hk, LANE), F32), pltpu.SemaphoreType.DMA(())],
        compiler_params=_cparams("arbitrary"),
        name="peer_outputs",
    )(idx, act, table)


def _final_kernel(h_ref, p_ref, g2_ref, gain_ref, o_ref):
    h = h_ref[0] + g2_ref[0] * p_ref[0]
    o_ref[0] = h * lax.rsqrt(jnp.mean(h * h, axis=-1, keepdims=True) + RMS_EPS) * gain_ref[...]


def final_norm_residual(h1, peer_out, g2, gain, tm=512):
    b, s, d = h1.shape
    row = pl.BlockSpec((1, tm, d), lambda i, j: (i, j, 0))
    return pl.pallas_call(
        _final_kernel,
        grid=(b, s // tm),
        in_specs=[row, row, pl.BlockSpec((1, 1, d), lambda i, j: (i, 0, 0)),
                  pl.BlockSpec((1, d), lambda i, j: (0, 0))],
        out_specs=row,
        out_shape=jax.ShapeDtypeStruct((b, s, d), F32),
        compiler_params=_cparams("parallel", "parallel"),
        name="final_norm",
    )(h1, peer_out, g2, gain.reshape(1, d))


def kernel(x, c, ctx, c_ctx, w_mod, b_mod, norm1, w_in, ssd_conv_w, ssd_conv_b, ssd_a_log, ssd_dt_bias, ssd_d, ssd_norm, hy_short_w, hy_short_b, hy_w1, hy_b1, hy_w2, hy_b2, hy_w3, hy_sin_freq, hy_filt_bias, hy_norm, w_out, norm2, peer_wq, peer_subkeys, peer_u, peer_v, final_norm):
    bsz, seq, d = x.shape
    ctx_len = ctx.shape[1]
    w_ssd = SSD_HEADS * HEAD_DIM
    gn = SSD_GROUPS * SSD_STATE
    conv_dim = w_ssd + 2 * gn
    off_dt = w_ssd + conv_dim
    off_hy = off_dt + 2 * SSD_HEADS
    w_hy = (w_in.shape[-1] - off_hy) // 3
    assert bsz == 2 and seq // GRID_W == 2 * FFT_N2 and w_hy == w_ssd

    i = 0
    c_rows = jnp.zeros((8, d), F32).at[:bsz].set(c).at[bsz].set(c_ctx)
    mod = modulation(c_rows, w_mod[i], b_mod[i])
    mod_l = mod[:bsz].reshape(bsz, 1, 6, d)
    mod_c = jnp.broadcast_to(mod[bsz].reshape(1, 1, 6, d), (bsz, 1, 6, d))
    sh1_l, sc1_l, g1_l, sh2_l, sc2_l, g2_l = [mod_l[:, :, j] for j in range(6)]
    sh1_c, sc1_c = mod_c[:, :, 0], mod_c[:, :, 1]

    wi = w_in[i].astype(BF16)
    wz, wx, wh = wi[:, :w_ssd], wi[:, w_ssd:off_dt], wi[:, off_hy:]
    wd = jnp.zeros((d, LANE), BF16).at[:, :2 * SSD_HEADS].set(wi[:, off_dt:off_hy])
    z_l, xbc_l, dt_l, hy_l = in_projection(x, sh1_l, sc1_l, norm1[i], wz, wx, wd, wh, tm=512)
    _, xbc_c, dt_c, _ = in_projection(ctx, sh1_c, sc1_c, norm1[i], wz, wx, wd, wh, tm=ctx_len)

    xa_l = dwconv(xbc_l, ssd_conv_w[i], ssd_conv_b[i], 0, conv_dim, True, BF16, tm=512)
    xa_c = dwconv(xbc_c, ssd_conv_w[i], ssd_conv_b[i], 0, conv_dim, True, BF16, tm=ctx_len)
    pad_row = lambda v: jnp.zeros((1, LANE), F32).at[0, :v.shape[0]].set(v)
    bias_row = pad_row(ssd_dt_bias[i].reshape(-1))
    a_row = pad_row(-jnp.exp(ssd_a_log[i].reshape(-1)))
    dskip = jnp.repeat(ssd_d[i], HEAD_DIM).reshape(1, w_ssd)
    h0 = jnp.zeros((bsz, SSD_GROUPS, SSD_STATE, SSD_HPG * HEAD_DIM), F32)
    _, s_f = ssd_scan(xa_c, dt_c, bias_row, a_row, dskip, h0, rev=False)
    _, s_b = ssd_scan(xa_c, dt_c, bias_row, a_row, dskip, h0, rev=True)
    y_f, _ = ssd_scan(xa_l, dt_l, bias_row, a_row, dskip, s_f, rev=False)
    y_b, _ = ssd_scan(xa_l, dt_l, bias_row, a_row, dskip, s_b, rev=True)

    v = dwconv(hy_l, hy_short_w[i][:, :w_hy], hy_short_b[i][:w_hy], 0, w_hy, False, F32, tm=512)
    x12 = dwconv(hy_l, hy_short_w[i][:, w_hy:], hy_short_b[i][w_hy:], w_hy, 2 * w_hy, False, BF16, tm=512)
    kfull = hyena_filters(seq, hy_w1[i], hy_b1[i], hy_w2[i], hy_b2[i], hy_w3[i], hy_sin_freq[i], w_hy)
    tab_n = _dft_tables(seq, colmajor=False)
    tab_c = _dft_tables(seq, colmajor=True)
    kspec = fft_inner_spectrum(fft_filter_outer(kfull, tab_n[2]), tab_n[3])
    y1 = long_conv(v, kspec, 0, tab_n, colmajor=False)
    zz = hyena_gate(y1, v, x12, hy_filt_bias[i, 0])
    y2 = long_conv(zz, kspec, w_hy, tab_c, colmajor=True)

    h1, f_mod, scores_t = mixer_output(
        x, y_f, y_b, z_l, y2, zz, x12, hy_filt_bias[i, 1], ssd_norm[i], hy_norm[i],
        w_out[i].astype(BF16), g1_l, norm2[i], sh2_l, sc2_l, peer_wq[i].astype(BF16), peer_subkeys[i])

    n_tok = bsz * seq
    idx, gate = peer_topk(scores_t)
    act = peer_activations(idx, f_mod.reshape(n_tok, d // LANE, LANE), gate, _pack_table(peer_u[i]))
    p_out = peer_outputs(idx, act, _pack_table(peer_v[i]))
    return final_norm_residual(h1, p_out.reshape(bsz, seq, d), g2_l, final_norm)
```

```python
import functools
import math

import jax
import jax.numpy as jnp
from jax import lax
from jax.experimental import pallas as pl
from jax.experimental.pallas import tpu as pltpu

F32 = jnp.float32
BF16 = jnp.bfloat16
HIGHEST = lax.Precision.HIGHEST

LANE = 128
VMEM_LIMIT = 48 * 1024 * 1024

RMS_EPS = 1e-6
GRID_W = 64
HEAD_DIM = 64
SSD_GROUPS = 4
SSD_HPG = 4
SSD_HEADS = SSD_GROUPS * SSD_HPG
SSD_STATE = 128
SSD_CHUNK = 128
SSD_CONV = 5
HY_SHORT = 3
HY_EMB = 33
HY_BANDS = (HY_EMB - 1) // 2
HY_TARGET = 1e-2
HY_MAX_DECAY = math.log(HY_TARGET) / 0.3
HY_MIN_DECAY = math.log(HY_TARGET) / 1.5
PEER_HEADS = 8
PEER_KEYS = 128
PEER_TOPK = 16
PEER_HALF = 128
FFT_N2 = 128
HALO = 16


def _cparams(*sem):
    return pltpu.CompilerParams(dimension_semantics=sem, vmem_limit_bytes=VMEM_LIMIT)


def _silu(x):
    return x * jax.nn.sigmoid(x)


def _mod_kernel(c_ref, w_ref, b_ref, o_ref):
    s = _silu(c_ref[...])
    o_ref[...] = jnp.dot(s, w_ref[...], precision=HIGHEST, preferred_element_type=F32) + b_ref[...]


def modulation(c_rows, w_mod, b_mod):
    d, n = w_mod.shape
    tn = 512
    return pl.pallas_call(
        _mod_kernel,
        grid=(n // tn,),
        in_specs=[pl.BlockSpec((8, d), lambda j: (0, 0)),
                  pl.BlockSpec((d, tn), lambda j: (0, j)),
                  pl.BlockSpec((1, tn), lambda j: (0, j))],
        out_specs=pl.BlockSpec((8, tn), lambda j: (0, j)),
        out_shape=jax.ShapeDtypeStruct((8, n), F32),
        compiler_params=_cparams("arbitrary"),
        name="modulation",
    )(c_rows, w_mod, b_mod.reshape(1, n))


def _inproj_kernel(x_ref, sh_ref, sc_ref, g_ref, wz_ref, wx_ref, wd_ref, wh_ref,
                   z_ref, xbc_ref, dt_ref, hy_ref):
    x = x_ref[0]
    xn = x * lax.rsqrt(jnp.mean(x * x, axis=-1, keepdims=True) + RMS_EPS) * g_ref[...]
    xm = (xn * (1.0 + sc_ref[0]) + sh_ref[0]).astype(BF16)
    z_ref[0] = jnp.dot(xm, wz_ref[...], preferred_element_type=F32).astype(z_ref.dtype)
    xbc_ref[0] = jnp.dot(xm, wx_ref[...], preferred_element_type=F32).astype(xbc_ref.dtype)
    dt_ref[0] = jnp.dot(xm, wd_ref[...], preferred_element_type=F32)
    hy_ref[0] = jnp.dot(xm, wh_ref[...], preferred_element_type=F32).astype(hy_ref.dtype)


def in_projection(x, shift, scale, gain, wz, wx, wd, wh, tm):
    b, s, d = x.shape
    const = lambda w: pl.BlockSpec(w.shape, lambda i, j: (0, 0), pipeline_mode=pl.Buffered(1))
    row = lambda n: pl.BlockSpec((1, tm, n), lambda i, j: (i, j, 0))
    vec = pl.BlockSpec((1, 1, d), lambda i, j: (i, 0, 0))
    return pl.pallas_call(
        _inproj_kernel,
        grid=(b, s // tm),
        in_specs=[row(d), vec, vec, pl.BlockSpec((1, d), lambda i, j: (0, 0)),
                  const(wz), const(wx), const(wd), const(wh)],
        out_specs=[row(wz.shape[1]), row(wx.shape[1]), row(wd.shape[1]), row(wh.shape[1])],
        out_shape=[jax.ShapeDtypeStruct((b, s, wz.shape[1]), BF16),
                   jax.ShapeDtypeStruct((b, s, wx.shape[1]), BF16),
                   jax.ShapeDtypeStruct((b, s, wd.shape[1]), F32),
                   jax.ShapeDtypeStruct((b, s, wh.shape[1]), BF16)],
        compiler_params=_cparams("parallel", "parallel"),
        name="in_projection",
    )(x, shift, scale, gain.reshape(1, d), wz, wx, wd, wh)


def _dwconv_kernel(prev_ref, main_ref, next_ref, w_ref, b_ref, o_ref, ext_ref, *, taps, act, tm, nt):
    i = pl.program_id(1)
    ext_ref[pl.ds(0, HALO), :] = jnp.where(i > 0, prev_ref[0].astype(F32), 0.0)
    ext_ref[pl.ds(HALO, tm), :] = main_ref[0].astype(F32)
    ext_ref[pl.ds(HALO + tm, HALO), :] = jnp.where(i < nt - 1, next_ref[0].astype(F32), 0.0)
    acc = jnp.broadcast_to(b_ref[...], (tm, b_ref.shape[1]))
    for k in range(taps):
        acc = acc + w_ref[pl.ds(k, 1), :] * ext_ref[pl.ds(HALO - taps // 2 + k, tm), :]
    if act:
        acc = _silu(acc)
    o_ref[0] = acc.astype(o_ref.dtype)


def dwconv(u, w, bias, col0, ncols, act, out_dtype, tm, cb=512):
    b, s, _ = u.shape
    taps = w.shape[0]
    nt = s // tm
    hb = tm // HALO
    c0 = col0 // cb
    w8 = jnp.zeros((8, ncols), F32).at[:taps].set(w)
    kern = functools.partial(_dwconv_kernel, taps=taps, act=act, tm=tm, nt=nt)
    return pl.pallas_call(
        kern,
        grid=(b, nt, ncols // cb),
        in_specs=[pl.BlockSpec((1, HALO, cb), lambda bi, i, j: (bi, jnp.maximum(i * hb - 1, 0), j + c0)),
                  pl.BlockSpec((1, tm, cb), lambda bi, i, j: (bi, i, j + c0)),
                  pl.BlockSpec((1, HALO, cb), lambda bi, i, j: (bi, jnp.minimum((i + 1) * hb, s // HALO - 1), j + c0)),
                  pl.BlockSpec((8, cb), lambda bi, i, j: (0, j)),
                  pl.BlockSpec((1, cb), lambda bi, i, j: (0, j))],
        out_specs=pl.BlockSpec((1, tm, cb), lambda bi, i, j: (bi, i, j)),
        out_shape=jax.ShapeDtypeStruct((b, s, ncols), out_dtype),
        scratch_shapes=[pltpu.VMEM((tm + 2 * HALO, cb), F32)],
        compiler_params=_cparams("parallel", "parallel", "parallel"),
        name="dwconv",
    )(u, u, u, w8, bias.reshape(1, ncols))


def _ssd_kernel(x_ref, b_ref, c_ref, dt_ref, bias_ref, a_ref, dskip_ref, h0_ref,
                y_ref, st_ref, *, rev, col0):
    q = SSD_CHUNK
    gw = SSD_HPG * HEAD_DIM

    @pl.when(pl.program_id(1) == 0)
    def _():
        st_ref[...] = h0_ref[...]

    dtp = jax.nn.softplus(dt_ref[0] + bias_ref[...])
    a = dtp * a_ref[...]
    li = lax.broadcasted_iota(jnp.int32, (q, q), 0)
    si = lax.broadcasted_iota(jnp.int32, (q, q), 1)
    tri = (si <= li).astype(F32)
    cum = jnp.dot(tri, a, precision=HIGHEST, preferred_element_type=F32)
    total = cum[q - 1:q, :]
    if rev:
        r = a - cum
        keep = si >= li
        d_out = jnp.exp(total + r)
        d_st = jnp.exp(-r)
    else:
        r = cum
        keep = li >= si
        d_out = jnp.exp(r)
        d_st = jnp.exp(total - r)
    r_t = jnp.transpose(r)
    d_tot = jnp.exp(total)
    lane_head = lax.broadcasted_iota(jnp.int32, (1, gw), 1) // HEAD_DIM

    x = x_ref[0].astype(F32)
    for g in range(SSD_GROUPS):
        cg = c_ref[0, :, g * SSD_STATE:(g + 1) * SSD_STATE]
        bg = b_ref[0, :, g * SSD_STATE:(g + 1) * SSD_STATE]
        cb = lax.dot_general(cg, bg, (((1,), (1,)), ((), ())), preferred_element_type=F32)
        xg = x[:, g * gw:(g + 1) * gw]
        dt_g = jnp.zeros((q, gw), F32)
        dout_g = jnp.zeros((q, gw), F32)
        dst_g = jnp.zeros((q, gw), F32)
        dtot_g = jnp.zeros((1, gw), F32)
        for hh in range(SSD_HPG):
            hc = col0 + g * SSD_HPG + hh
            m = lane_head == hh
            dt_g = jnp.where(m, dtp[:, hc:hc + 1], dt_g)
            dout_g = jnp.where(m, d_out[:, hc:hc + 1], dout_g)
            dst_g = jnp.where(m, d_st[:, hc:hc + 1], dst_g)
            dtot_g = jnp.where(m, d_tot[:, hc:hc + 1], dtot_g)
        xdt = xg * dt_g
        st_old = st_ref[0, g]
        y_g = jnp.dot(cg, st_old.astype(BF16), preferred_element_type=F32) * dout_g
        for hh in range(SSD_HPG):
            hc = col0 + g * SSD_HPG + hh
            diff = r[:, hc:hc + 1] - r_t[hc:hc + 1, :]
            lm = jnp.where(keep, jnp.exp(jnp.minimum(diff, 0.0)), 0.0)
            mh = (cb * lm).astype(BF16)
            xh = jnp.where(lane_head == hh, xdt, 0.0).astype(BF16)
            y_g = y_g + jnp.dot(mh, xh, preferred_element_type=F32)
        if not rev:
            y_g = y_g + xg * dskip_ref[:, g * gw:(g + 1) * gw]
        y_ref[0, :, g * gw:(g + 1) * gw] = y_g.astype(y_ref.dtype)
        upd = lax.dot_general(bg, (xdt * dst_g).astype(BF16), (((0,), (0,)), ((), ())),
                              preferred_element_type=F32)
        st_ref[0, g] = st_old * dtot_g + upd


def ssd_scan(xbc, dt_raw, dt_bias_row, a_row, dskip_row, h0, rev):
    b, s, _ = xbc.shape
    w = SSD_HEADS * HEAD_DIM
    gn = SSD_GROUPS * SSD_STATE
    nc = s // SSD_CHUNK
    cidx = (lambda c: nc - 1 - c) if rev else (lambda c: c)
    kern = functools.partial(_ssd_kernel, rev=rev, col0=SSD_HEADS if rev else 0)
    st_spec = pl.BlockSpec((1, SSD_GROUPS, SSD_STATE, SSD_HPG * HEAD_DIM), lambda bi, c: (bi, 0, 0, 0))
    return pl.pallas_call(
        kern,
        grid=(b, nc),
        in_specs=[pl.BlockSpec((1, SSD_CHUNK, w), lambda bi, c: (bi, cidx(c), 0)),
                  pl.BlockSpec((1, SSD_CHUNK, gn), lambda bi, c: (bi, cidx(c), w // gn)),
                  pl.BlockSpec((1, SSD_CHUNK, gn), lambda bi, c: (bi, cidx(c), w // gn + 1)),
                  pl.BlockSpec((1, SSD_CHUNK, LANE), lambda bi, c: (bi, cidx(c), 0)),
                  pl.BlockSpec((1, LANE), lambda bi, c: (0, 0)),
                  pl.BlockSpec((1, LANE), lambda bi, c: (0, 0)),
                  pl.BlockSpec((1, w), lambda bi, c: (0, 0)),
                  st_spec],
        out_specs=[pl.BlockSpec((1, SSD_CHUNK, w), lambda bi, c: (bi, cidx(c), 0)), st_spec],
        out_shape=[jax.ShapeDtypeStruct((b, s, w), BF16),
                   jax.ShapeDtypeStruct(h0.shape, F32)],
        compiler_params=_cparams("parallel", "arbitrary"),
        name="ssd_scan_rev" if rev else "ssd_scan_fwd",
    )(xbc, xbc, xbc, dt_raw, dt_bias_row, a_row, dskip_row, h0)


def _filter_kernel(z_ref, w1_ref, b1_ref, w2_ref, b2_ref, w3_ref, f0_ref, f1_ref, dl_ref, o_ref, *, seqlen):
    z = z_ref[...]
    h = jnp.sin(f0_ref[...] * (jnp.dot(z, w1_ref[...], precision=HIGHEST, preferred_element_type=F32) + b1_ref[...]))
    h = jnp.sin(f1_ref[...] * (jnp.dot(h, w2_ref[...], precision=HIGHEST, preferred_element_type=F32) + b2_ref[...]))
    w3 = w3_ref[...]
    h_hi, w_hi = h.astype(BF16), w3.astype(BF16)
    h_lo = (h - h_hi.astype(F32)).astype(BF16)
    w_lo = (w3 - w_hi.astype(F32)).astype(BF16)
    h = (jnp.dot(h_hi, w_hi, preferred_element_type=F32) + jnp.dot(h_hi, w_lo, preferred_element_type=F32)
         + jnp.dot(h_lo, w_hi, preferred_element_type=F32))
    tl = z.shape[0]
    n = pl.program_id(0) * tl + lax.broadcasted_iota(jnp.int32, (tl, 1), 0)
    o_ref[...] = jnp.where(n == seqlen, 0.0, h * jnp.exp(-z[:, 0:1] * dl_ref[...]))


def hyena_filters(seqlen, w1, b1, w2, b2, w3, sin_freq, width):
    t = jnp.linspace(0.0, 1.0, seqlen, dtype=F32)[:, None]
    w_ang = 2.0 * math.pi * jnp.arange(seqlen, dtype=F32) / seqlen
    bands = jnp.linspace(1e-4, HY_BANDS - 1, HY_BANDS, dtype=F32)
    ang = w_ang[:, None] * bands[None, :]
    zpos = jnp.concatenate([t, jnp.cos(ang), -jnp.sin(ang)], axis=-1)
    zboth = jnp.concatenate([zpos, jnp.zeros_like(zpos[:1]), zpos[:0:-1]], axis=0)
    zpad = jnp.zeros((2 * seqlen, LANE), F32).at[:, :HY_EMB].set(zboth)
    hid = w1.shape[1]
    w1p = jnp.zeros((LANE, hid), F32).at[:HY_EMB].set(w1)
    deltas = jnp.abs(jnp.linspace(HY_MIN_DECAY, HY_MAX_DECAY, width, dtype=F32))
    dl = jnp.tile(deltas, 2).reshape(1, 2 * width)
    tl = 512
    nfwd = seqlen // tl
    full = lambda a: pl.BlockSpec(a.shape, lambda i, j: (0, 0))
    return pl.pallas_call(
        functools.partial(_filter_kernel, seqlen=seqlen),
        grid=(2 * seqlen // tl, 2),
        in_specs=[pl.BlockSpec((tl, LANE), lambda i, j: (i, 0)),
                  full(w1p), pl.BlockSpec((1, hid), lambda i, j: (0, 0)),
                  full(w2), pl.BlockSpec((1, hid), lambda i, j: (0, 0)),
                  pl.BlockSpec((hid, width), lambda i, j: (0, 2 * j + i // nfwd)),
                  pl.BlockSpec((1, hid), lambda i, j: (0, 0)),
                  pl.BlockSpec((1, hid), lambda i, j: (0, 0)),
                  pl.BlockSpec((1, width), lambda i, j: (0, j))],
        out_specs=pl.BlockSpec((tl, width), lambda i, j: (i, j)),
        out_shape=jax.ShapeDtypeStruct((2 * seqlen, 2 * width), F32),
        compiler_params=_cparams("parallel", "parallel"),
        name="hyena_filters",
    )(zpad, w1p, b1.reshape(1, hid), w2, b2.reshape(1, hid), w3,
      sin_freq[0].reshape(1, hid), sin_freq[1].reshape(1, hid), dl)


def _dft_tables(seqlen, colmajor):
    n = 2 * seqlen
    n2 = FFT_N2
    n1 = n // n2
    h1 = n1 // 2
    k1 = jnp.arange(n1, dtype=jnp.int32)
    m1 = jnp.arange(h1, dtype=jnp.int32)
    if colmajor:
        m1 = 2 * (m1 % GRID_W) + m1 // GRID_W
    ph = (k1[:, None] * m1[None, :] * n2) % n
    ang = (-2.0 * math.pi / n) * ph.astype(F32)
    wr, wi = jnp.cos(ang), jnp.sin(ang)
    w_fwd = jnp.concatenate([jnp.concatenate([wr, -wi], 1), jnp.concatenate([wi, wr], 1)], 0)
    w_inv = jnp.concatenate([jnp.concatenate([wr.T, wi.T], 1), jnp.concatenate([-wi.T, wr.T], 1)], 0) / n
    mf = jnp.arange(n1, dtype=jnp.int32)
    phf = (k1[:, None] * mf[None, :] * n2) % n
    angf = (-2.0 * math.pi / n) * phf.astype(F32)
    w_flt = jnp.concatenate([jnp.cos(angf), jnp.sin(angf)], 0)
    a2 = jnp.arange(n2, dtype=jnp.int32)
    ph2 = (a2[None, :, None] * a2[None, None, :] * n1 + k1[:, None, None] * a2[None, None, :]) % n
    ang2 = (-2.0 * math.pi / n) * ph2.astype(F32)
    c2, s2 = jnp.cos(ang2), jnp.sin(ang2)
    w2_fwd = jnp.concatenate([jnp.concatenate([c2, -s2], 2), jnp.concatenate([s2, c2], 2)], 1)
    c2t, s2t = jnp.swapaxes(c2, 1, 2), jnp.swapaxes(s2, 1, 2)
    w2_inv = jnp.concatenate([jnp.concatenate([c2t, s2t], 2), jnp.concatenate([-s2t, c2t], 2)], 1)
    return (w_fwd.astype(BF16), w_inv.astype(BF16), w_flt.astype(BF16),
            w2_fwd.astype(BF16), w2_inv.astype(BF16))


def _fft_outer_fwd_kernel(x_ref, w_ref, o_ref, *, colmajor, nb):
    n1 = o_ref.shape[0]
    for j in range(8):
        parts = []
        for bi in range(nb):
            if colmajor:
                parts.append(x_ref[bi, 0, pl.ds(j * GRID_W, GRID_W), :].astype(F32))
                parts.append(x_ref[bi, 1, pl.ds(j * GRID_W, GRID_W), :].astype(F32))
            else:
                parts.append(x_ref[bi, :, j, :])
        xs = jnp.concatenate(parts, axis=0).astype(BF16)
        a = jnp.dot(w_ref[...], xs, preferred_element_type=F32)
        o_ref[:, 0, j, :] = a[:n1]
        o_ref[:, 1, j, :] = a[n1:]


def fft_outer_fwd(x, w, colmajor, cblk=512):
    b, seqlen, c = x.shape
    n2 = FFT_N2
    n1 = 2 * seqlen // n2
    if colmajor:
        xv = x.reshape(b, 2, seqlen // 2, c)
        in_spec = pl.BlockSpec((b, 2, 8 * GRID_W, cblk), lambda g, j: (0, 0, g, j))
    else:
        xv = x.reshape(b, n1 // 2, n2, c)
        in_spec = pl.BlockSpec((b, n1 // 2, 8, cblk), lambda g, j: (0, 0, g, j))
    kern = functools.partial(_fft_outer_fwd_kernel, colmajor=colmajor, nb=b)
    return pl.pallas_call(
        kern,
        grid=(n2 // 8, c // cblk),
        in_specs=[in_spec, pl.BlockSpec(w.shape, lambda g, j: (0, 0))],
        out_specs=pl.BlockSpec((n1, 2, 8, cblk), lambda g, j: (0, 0, g, j)),
        out_shape=jax.ShapeDtypeStruct((n1, 2, n2, c), F32),
        compiler_params=_cparams("parallel", "parallel"),
        name="fft_outer_fwd_cm" if colmajor else "fft_outer_fwd",
    )(xv, w)


def _fft_filter_outer_kernel(k_ref, w_ref, o_ref):
    n1 = o_ref.shape[0]
    for j in range(8):
        a = jnp.dot(w_ref[...], k_ref[:, j, :].astype(BF16), preferred_element_type=F32)
        o_ref[:, 0, j, :] = a[:n1]
        o_ref[:, 1, j, :] = a[n1:]


def fft_filter_outer(k, w, cblk=512):
    n, c = k.shape
    n2 = FFT_N2
    n1 = n // n2
    return pl.pallas_call(
        _fft_filter_outer_kernel,
        grid=(n2 // 8, c // cblk),
        in_specs=[pl.BlockSpec((n1, 8, cblk), lambda g, j: (0, g, j)),
                  pl.BlockSpec(w.shape, lambda g, j: (0, 0))],
        out_specs=pl.BlockSpec((n1, 2, 8, cblk), lambda g, j: (0, 0, g, j)),
        out_shape=jax.ShapeDtypeStruct((n1, 2, n2, c), F32),
        compiler_params=_cparams("parallel", "parallel"),
        name="fft_filter_outer",
    )(k.reshape(n1, n2, c), w)


def _fft_inner_spec_kernel(a_ref, wf_ref, o_ref):
    n2 = FFT_N2
    cb = a_ref.shape[-1]
    for j in range(8):
        xs = a_ref[j].reshape(2 * n2, cb).astype(BF16)
        o_ref[j] = jnp.dot(wf_ref[j], xs, preferred_element_type=F32).reshape(2, n2, cb)


def fft_inner_spectrum(a, w2_fwd, cblk=512):
    n1, _, n2, c = a.shape
    blk = pl.BlockSpec((8, 2, n2, cblk), lambda g, j: (g, 0, 0, j))
    return pl.pallas_call(
        _fft_inner_spec_kernel,
        grid=(n1 // 8, c // cblk),
        in_specs=[blk, pl.BlockSpec((8, 2 * n2, 2 * n2), lambda g, j: (g, 0, 0))],
        out_specs=blk,
        out_shape=jax.ShapeDtypeStruct(a.shape, F32),
        compiler_params=_cparams("parallel", "arbitrary"),
        name="fft_inner_spectrum",
    )(a, w2_fwd)


def _fft_inner_conv_kernel(a_ref, k_ref, wf_ref, wi_ref, o_ref):
    n2 = FFT_N2
    cb = a_ref.shape[-1]
    for j in range(8):
        xs = a_ref[j].reshape(2 * n2, cb).astype(BF16)
        s = jnp.dot(wf_ref[j], xs, preferred_element_type=F32)
        sr, si = s[:n2], s[n2:]
        kr, ki = k_ref[j, 0], k_ref[j, 1]
        y = jnp.concatenate([sr * kr - si * ki, sr * ki + si * kr], axis=0).astype(BF16)
        z = jnp.dot(wi_ref[j], y, preferred_element_type=F32)
        o_ref[:, 0, j, :] = z[:n2]
        o_ref[:, 1, j, :] = z[n2:]


def fft_inner_conv(a, kspec, kcol0, w2_fwd, w2_inv, cblk=512):
    n1, _, n2, c = a.shape
    kc = kcol0 // cblk
    wspec = pl.BlockSpec((8, 2 * n2, 2 * n2), lambda g, j: (g, 0, 0))
    return pl.pallas_call(
        _fft_inner_conv_kernel,
        grid=(n1 // 8, c // cblk),
        in_specs=[pl.BlockSpec((8, 2, n2, cblk), lambda g, j: (g, 0, 0, j)),
                  pl.BlockSpec((8, 2, n2, cblk), lambda g, j: (g, 0, 0, j + kc)),
                  wspec, wspec],
        out_specs=pl.BlockSpec((n2, 2, 8, cblk), lambda g, j: (0, 0, g, j)),
        out_shape=jax.ShapeDtypeStruct((n2, 2, n1, c), F32),
        compiler_params=_cparams("parallel", "arbitrary"),
        name="fft_inner_conv",
    )(a, kspec, w2_fwd, w2_inv)


def _fft_outer_inv_kernel(b_ref, w_ref, o_ref, *, colmajor, nb):
    cb = b_ref.shape[-1]
    n1 = b_ref.shape[2]
    h1 = n1 // 2
    for j in range(8):
        xs = b_ref[j].reshape(2 * n1, cb).astype(BF16)
        y = jnp.dot(w_ref[...], xs, preferred_element_type=F32)
        for bi in range(nb):
            yb = y[bi * h1:(bi + 1) * h1]
            if colmajor:
                o_ref[bi, 0, pl.ds(j * GRID_W, GRID_W), :] = yb[:GRID_W]
                o_ref[bi, 1, pl.ds(j * GRID_W, GRID_W), :] = yb[GRID_W:]
            else:
                o_ref[bi, :, j, :] = yb


def fft_outer_inv(bsp, w, nb, colmajor, cblk=512):
    n2, _, n1, c = bsp.shape
    seqlen = n1 * n2 // 2
    if colmajor:
        out_spec = pl.BlockSpec((nb, 2, 8 * GRID_W, cblk), lambda g, j: (0, 0, g, j))
        out_shape = jax.ShapeDtypeStruct((nb, 2, seqlen // 2, c), F32)
    else:
        out_spec = pl.BlockSpec((nb, n1 // 2, 8, cblk), lambda g, j: (0, 0, g, j))
        out_shape = jax.ShapeDtypeStruct((nb, n1 // 2, n2, c), F32)
    kern = functools.partial(_fft_outer_inv_kernel, colmajor=colmajor, nb=nb)
    y = pl.pallas_call(
        kern,
        grid=(n2 // 8, c // cblk),
        in_specs=[pl.BlockSpec((8, 2, n1, cblk), lambda g, j: (g, 0, 0, j)),
                  pl.BlockSpec(w.shape, lambda g, j: (0, 0))],
        out_specs=out_spec,
        out_shape=out_shape,
        compiler_params=_cparams("parallel", "parallel"),
        name="fft_outer_inv_cm" if colmajor else "fft_outer_inv",
    )(bsp, w)
    return y.reshape(nb, seqlen, c)


def long_conv(u, kspec, kcol0, tables, colmajor):
    w_fwd, w_inv, _, w2_fwd, w2_inv = tables
    a = fft_outer_fwd(u, w_fwd, colmajor)
    bsp = fft_inner_conv(a, kspec, kcol0, w2_fwd, w2_inv)
    return fft_outer_inv(bsp, w_inv, u.shape[0], colmajor)


def _hy_gate_kernel(y_ref, v_ref, x1_ref, fb_ref, o_ref):
    o_ref[0] = x1_ref[0].astype(F32) * (y_ref[0] + v_ref[0] * fb_ref[...])


def hyena_gate(y, v, x12, fbias, tm=512, cb=512):
    b, s, c = y.shape
    blk = lambda: pl.BlockSpec((1, tm, cb), lambda bi, i, j: (bi, i, j))
    return pl.pallas_call(
        _hy_gate_kernel,
        grid=(b, s // tm, c // cb),
        in_specs=[blk(), blk(), blk(), pl.BlockSpec((1, cb), lambda bi, i, j: (0, j))],
        out_specs=blk(),
        out_shape=jax.ShapeDtypeStruct((b, s, c), F32),
        compiler_params=_cparams("parallel", "parallel", "parallel"),
        name="hyena_gate",
    )(y, v, x12, fbias.reshape(1, c))


def _mix_kernel(x_ref, yf_ref, yb_ref, z_ref, y2_ref, zz_ref, x12_ref,
                fb_ref, gs_ref, gh_ref, mh_ref, wo_ref, g1_ref, n2_ref, sh2_ref, sc2_ref,
                wq_ref, sk_ref, h1_ref, f_ref, st_ref):
    w = yf_ref.shape[-1]
    gw = w // SSD_GROUPS
    ys = (yf_ref[0].astype(F32) + yb_ref[0].astype(F32)) * _silu(z_ref[0].astype(F32))
    parts = []
    for g in range(SSD_GROUPS):
        blk = ys[:, g * gw:(g + 1) * gw]
        ms = jnp.mean(blk * blk, axis=-1, keepdims=True)
        parts.append(blk * lax.rsqrt(ms + RMS_EPS))
    o_s = jnp.concatenate(parts, axis=1) * gs_ref[...]
    yh = x12_ref[0].astype(F32) * (y2_ref[0] + zz_ref[0] * fb_ref[...])
    parts = []
    for c in range(w // LANE):
        blk = yh[:, c * LANE:(c + 1) * LANE]
        sq = blk * blk
        hi = sq.astype(BF16)
        lo = (sq - hi.astype(F32)).astype(BF16)
        ms = (jnp.dot(hi, mh_ref[...], preferred_element_type=F32)
              + jnp.dot(lo, mh_ref[...], preferred_element_type=F32)) * (1.0 / HEAD_DIM)
        parts.append(blk * lax.rsqrt(ms + RMS_EPS))
    o_h = jnp.concatenate(parts, axis=1) * gh_ref[...]
    mix = (jnp.dot(o_s.astype(BF16), wo_ref[pl.ds(0, w), :], preferred_element_type=F32)
           + jnp.dot(o_h.astype(BF16), wo_ref[pl.ds(w, w), :], preferred_element_type=F32))
    h1 = x_ref[0] + g1_ref[0] * mix
    h1_ref[0] = h1
    hn = h1 * lax.rsqrt(jnp.mean(h1 * h1, axis=-1, keepdims=True) + RMS_EPS) * n2_ref[...]
    f = hn * (1.0 + sc2_ref[0]) + sh2_ref[0]
    f_ref[0] = f
    qv =jnp.dot(f.astype(BF16), wq_ref[...], preferred_element_type=F32).astype(BF16)
    for blk in range(2 * PEER_HEADS):
        qb = qv[:, blk * PEER_HALF:(blk + 1) * PEER_HALF]
        st_ref[blk] = lax.dot_general(sk_ref[blk % 2], qb, (((1,), (1,)), ((), ())),
                                      preferred_element_type=F32)


def mixer_output(x, y_f, y_b, z, y2, zz, x12, fbias, g_ssd, g_hy, w_out, g1, norm2, sh2, sc2, wq, subkeys, tm=256):
    b, s, d = x.shape
    w = y_f.shape[-1]
    nblk = 2 * PEER_HEADS
    lane = jnp.arange(LANE)
    m_h = (lane[:, None] // HEAD_DIM == lane[None, :] // HEAD_DIM).astype(BF16)
    row = lambda n, c0=0: pl.BlockSpec((1, tm, n), lambda i, j: (i, j, c0))
    vec = lambda n: pl.BlockSpec((1, 1, n), lambda i, j: (i, 0, 0))
    const = lambda a: pl.BlockSpec(a.shape, lambda i, j: (0,) * a.ndim, pipeline_mode=pl.Buffered(1))
    sk = subkeys.astype(BF16)
    return pl.pallas_call(
        _mix_kernel,
        grid=(b, s // tm),
        in_specs=[row(d), row(w), row(w), row(w), row(w), row(w), row(w, 1),
                  pl.BlockSpec((1, w), lambda i, j: (0, 0)),
                  pl.BlockSpec((1, w), lambda i, j: (0, 0)),
                  pl.BlockSpec((1, w), lambda i, j: (0, 0)),
                  const(m_h), const(w_out), vec(d),
                  pl.BlockSpec((1, d), lambda i, j: (0, 0)), vec(d), vec(d),
                  const(wq), const(sk)],
        out_specs=[row(d), row(d),
                   pl.BlockSpec((nblk, PEER_KEYS, tm), lambda i, j: (0, 0, i * (s // tm) + j))],
        out_shape=[jax.ShapeDtypeStruct((b, s, d), F32),
                   jax.ShapeDtypeStruct((b, s, d), F32),
                   jax.ShapeDtypeStruct((nblk, PEER_KEYS, b * s), F32)],
        compiler_params=_cparams("parallel", "parallel"),
        name="mixer_output",
    )(x, y_f, y_b, z, y2, zz, x12, fbias.reshape(1, w), g_ssd.reshape(1, w), g_hy.reshape(1, w),
      m_h, w_out, g1, norm2.reshape(1, d), sh2, sc2, wq, sk)


def _topk_rows(x, riota, k):
    t = x.shape[1]
    slot = lax.broadcasted_iota(jnp.int32, (k, t), 0)
    vals = jnp.zeros((k, t), F32)
    idxs = jnp.zeros((k, t), F32)
    big = jnp.float32(1e9)
    for it in range(k):
        m = jnp.max(x, axis=0, keepdims=True)
        sel = jnp.min(jnp.where(x == m, riota, big), axis=0, keepdims=True)
        vals = jnp.where(slot == it, m, vals)
        idxs = jnp.where(slot == it, sel, idxs)
        x = jnp.where(riota == sel, -jnp.inf, x)
    return vals, idxs


def _pair_candidates(v1, v2, k):
    assert k == 16
    t = v1.shape[1]
    b16 = lax.broadcasted_iota(jnp.int32, (k, t), 0)
    b8 = lax.broadcasted_iota(jnp.int32, (8, t), 0)
    vals = [v1[0:1, :] + v2]
    ids = [b16.astype(F32)]
    for a in range(1, 8):
        vals.append(jnp.where(b8 < k // (a + 1), v1[a:a + 1, :] + v2[0:8, :], -jnp.inf))
        ids.append((b8 + a * k).astype(F32))
    vals.append(v1[8:16, :] + v2[0:1, :])
    ids.append(((b8 + 8) * k).astype(F32))
    return jnp.concatenate(vals, axis=0), jnp.concatenate(ids, axis=0)


def _topk_kernel(s_ref, idx_ref, gate_ref, idx_t, gate_t):
    k = PEER_TOPK
    t = s_ref.shape[-1]
    key_iota = lax.broadcasted_iota(jnp.int32, (PEER_KEYS, t), 0).astype(F32)
    r16 = lax.broadcasted_iota(jnp.int32, (k, t), 0).astype(F32)
    slot = lax.broadcasted_iota(jnp.int32, (k, t), 0)

    def subkey_topk(h):
        v1, i1 = _topk_rows(s_ref[2 * h], key_iota, k)
        v2, i2 = _topk_rows(s_ref[2 * h + 1], key_iota, k)
        return v1, i1, v2, i2

    def finish(h, v1, i1, v2, i2):
        cand, cid = _pair_candidates(v1, v2, k)
        sc, ci = _topk_rows(cand, cid, k)
        e = jnp.exp(sc - sc[0:1, :])
        ih = jnp.zeros((k, t), F32)
        for j in range(k):
            cj = ci[j:j + 1, :]
            a = jnp.floor(cj * (1.0 / k))
            bb = cj - a * k
            e1 = jnp.sum(jnp.where(r16 == a, i1, 0.0), axis=0, keepdims=True)
            e2 = jnp.sum(jnp.where(r16 == bb, i2, 0.0), axis=0, keepdims=True)
            ih = jnp.where(slot == j, e1 * PEER_KEYS + e2, ih)
        row0 = pl.multiple_of(h * k, k)
        idx_t[pl.ds(row0, k), :] = ih
        gate_t[pl.ds(row0, k), :] = e / jnp.sum(e, axis=0, keepdims=True)

    def body(h, prev):
        finish(h - 1, *prev)
        return subkey_topk(h)

    last = lax.fori_loop(1, PEER_HEADS, body, subkey_topk(0))
    finish(PEER_HEADS - 1, *last)
    idx_ref[...] = jnp.transpose(idx_t[...]).astype(jnp.int32)
    gate_ref[...] = jnp.transpose(gate_t[...])


def peer_topk(scores_t, tt=128):
    nblk, nk, n = scores_t.shape
    hk = PEER_HEADS * PEER_TOPK
    return pl.pallas_call(
        _topk_kernel,
        grid=(n // tt,),
        in_specs=[pl.BlockSpec((nblk, nk, tt), lambda i: (0, 0, i))],
        out_specs=[pl.BlockSpec((tt, hk), lambda i: (i, 0)), pl.BlockSpec((tt, hk), lambda i: (i, 0))],
        out_shape=[jax.ShapeDtypeStruct((n, hk), jnp.int32), jax.ShapeDtypeStruct((n, hk), F32)],
        scratch_shapes=[pltpu.VMEM((hk, tt), F32), pltpu.VMEM((hk, tt), F32)],
        compiler_params=_cparams("parallel"),
        name="peer_topk",
    )(scores_t)


HALF_SUB = 4


def _pack_table(tab):
    e, d = tab.shape
    bits = lax.bitcast_convert_type(tab.astype(BF16), jnp.uint16).astype(jnp.uint32)
    return ((bits[:, d // 2:] << 16) | bits[:, :d // 2]).reshape(e, d // (2 * LANE), LANE)


def _pair_rows(tab_vmem, idx_ref, t, ka, kb):
    row = idx_ref.at[t]
    return jnp.concatenate([tab_vmem[row[ka]], tab_vmem[row[kb]]], axis=0)


def _unpack(wd):
    lo = pltpu.bitcast(wd << 16, F32)
    hi = pltpu.bitcast(wd & jnp.uint32(0xFFFF0000), F32)
    return lo, hi


def _load_table_once(tab_hbm, tab_vmem, sem):
    @pl.when(pl.program_id(0) == 0)
    def _():
        cp = pltpu.make_async_copy(tab_hbm, tab_vmem, sem)
        cp.start()
        cp.wait()


def _peer_act_kernel(idx_ref, f_ref, gate_ref, tab_hbm, act_ref, tab_vmem, part_ref, sem, *, tok):
    _load_table_once(tab_hbm, tab_vmem, sem)
    hk = gate_ref.shape[-1]
    sub = lax.broadcasted_iota(jnp.int32, (8, LANE), 0)
    low = sub < HALF_SUB
    eye = (lax.broadcasted_iota(jnp.int32, (hk, LANE), 0)
           == lax.broadcasted_iota(jnp.int32, (hk, LANE), 1))

    def split_f(t):
        f8 = f_ref[t]
        fsw = pltpu.roll(f8, HALF_SUB, axis=0)
        return jnp.where(low, f8, fsw), jnp.where(low, fsw, f8)

    def partial_sums(i, carry):
        toks = (2 * i, 2 * i + 1)
        fs = [split_f(t) for t in toks]
        for g in range(hk // 8):
            for t, (flo, fhi) in zip(toks, fs):
                merged = None
                for j in range(HALF_SUB):
                    lo, hi = _unpack(_pair_rows(tab_vmem, idx_ref, t, 8 * g + j, 8 * g + HALF_SUB + j))
                    p = lo * flo + hi * fhi
                    p = p + pltpu.roll(p, 6, axis=0)
                    p = p + pltpu.roll(p, 7, axis=0)
                    if j:
                        p = pltpu.roll(p, j, axis=0)
                    merged = p if merged is None else jnp.where((sub & (HALF_SUB - 1)) == j, p, merged)
                part_ref[t, pl.ds(8 * g, 8), :] = merged
        return carry

    def lane_sums(i, carry):
        rows = []
        for s8 in range(8):
            s = jnp.sum(part_ref[8 * i + s8], axis=1, keepdims=True)
            rows.append(jnp.sum(jnp.where(eye, s, 0.0), axis=0, keepdims=True))
        act_ref[pl.ds(pl.multiple_of(8 * i, 8), 8), :] = jnp.concatenate(rows, axis=0)
        return carry

    lax.fori_loop(0, tok // 2, partial_sums, 0)
    lax.fori_loop(0, tok // 8, lane_sums, 0)
    pre = act_ref[...]
    act_ref[...] = 0.5 * pre * (1.0 + lax.erf(pre * (1.0 / math.sqrt(2.0)))) * gate_ref[...]


def peer_activations(idx, f8, gate, table, tok=64):
    n, hk = idx.shape
    kern = functools.partial(_peer_act_kernel, tok=tok)
    return pl.pallas_call(
        kern,
        grid=(n // tok,),
        in_specs=[pl.BlockSpec((tok, hk), lambda i: (i, 0), memory_space=pltpu.SMEM),
                  pl.BlockSpec((tok, 8, LANE), lambda i: (i, 0, 0)),
                  pl.BlockSpec((tok, hk), lambda i: (i, 0)),
                  pl.BlockSpec(memory_space=pl.ANY)],
        out_specs=pl.BlockSpec((tok, hk), lambda i: (i, 0)),
        out_shape=jax.ShapeDtypeStruct((n, hk), F32),
        scratch_shapes=[pltpu.VMEM(table.shape, table.dtype), pltpu.VMEM((tok, hk, LANE), F32),
                        pltpu.SemaphoreType.DMA(())],
        compiler_params=_cparams("arbitrary"),
        name="peer_activations",
    )(idx, f8, gate, table)


def _peer_out_kernel(idx_ref, act_ref, tab_hbm, o_ref, tab_vmem, splat_a, splat_b, sem, *, tok):
    _load_table_once(tab_hbm, tab_vmem, sem)
    hk = idx_ref.shape[-1]
    low = lax.broadcasted_iota(jnp.int32, (8, LANE), 0) < HALF_SUB
    eye = (lax.broadcasted_iota(jnp.int32, (hk, LANE), 0)
           == lax.broadcasted_iota(jnp.int32, (hk, LANE), 1))
    ones = jnp.ones((LANE, LANE), BF16)

    def make_splat(ref, t):
        for s in range(2):
            diag = jnp.where(eye, act_ref[pl.ds(t + s, 1), :], 0.0)
            d_hi = diag.astype(BF16)
            d_lo = (diag - d_hi.astype(F32)).astype(BF16)
            ref[s] = (jnp.dot(d_hi, ones, preferred_element_type=F32)
                      + jnp.dot(d_lo, ones, preferred_element_type=F32))

    def accumulate(ref, t0):
        for s in range(2):
            t = t0 + s
            acc_lo = [jnp.zeros((8, LANE), F32) for _ in range(2)]
            acc_hi = [jnp.zeros((8, LANE), F32) for _ in range(2)]
            for j in range(hk // 2):
                lo, hi = _unpack(_pair_rows(tab_vmem, idx_ref, t, 2 * j, 2 * j + 1))
                a8 = jnp.where(low, ref[s, pl.ds(2 * j, 1), :], ref[s, pl.ds(2 * j + 1, 1), :])
                acc_lo[j % 2] = acc_lo[j % 2] + a8 * lo
                acc_hi[j % 2] = acc_hi[j % 2] + a8 * hi
            lo8 = acc_lo[0] + acc_lo[1]
            hi8 = acc_hi[0] + acc_hi[1]
            lo8 = lo8 + pltpu.roll(lo8, HALF_SUB, axis=0)
            hi8 = hi8 + pltpu.roll(hi8, HALF_SUB, axis=0)
            o_ref[t] = jnp.where(low, lo8, hi8)

    def body(i, carry):
        t0 = 4 * i
        make_splat(splat_b, t0 + 2)
        accumulate(splat_a, t0)
        make_splat(splat_a, jnp.minimum(t0 + 4, tok - 2))
        accumulate(splat_b, t0 + 2)
        return carry

    make_splat(splat_a, 0)
    lax.fori_loop(0, tok // 4, body, 0)


def peer_outputs(idx, act, table, tok=64):
    n, hk = idx.shape
    kern = functools.partial(_peer_out_kernel, tok=tok)
    return pl.pallas_call(
        kern,
        grid=(n // tok,),
        in_specs=[pl.BlockSpec((tok, hk), lambda i: (i, 0), memory_space=pltpu.SMEM),
                  pl.BlockSpec((tok, hk), lambda i: (i, 0)),
                  pl.BlockSpec(memory_space=pl.ANY)],
        out_specs=pl.BlockSpec((tok, 8, LANE), lambda i: (i, 0, 0)),
        out_shape=jax.ShapeDtypeStruct((n, 8, LANE), F32),
        scratch_shapes=[pltpu.VMEM(table.shape, table.dtype), pltpu.VMEM((2, hk, LANE), F32),
                        pltpu.VMEM((2, hk, LANE), F32), pltpu.SemaphoreType.DMA(())],
        compiler_params=_cparams("arbitrary"),
        name="peer_outputs",
    )(idx, act, table)


def _final_kernel(h_ref, p_ref, g2_ref, gain_ref, o_ref):
    h = h_ref[0] + g2_ref[0] * p_ref[0]
    o_ref[0] = h * lax.rsqrt(jnp.mean(h * h, axis=-1, keepdims=True) + RMS_EPS) * gain_ref[...]


def final_norm_residual(h1, peer_out, g2, gain, tm=512):
    b, s, d = h1.shape
    row = pl.BlockSpec((1, tm, d), lambda i, j: (i, j, 0))
    return pl.pallas_call(
        _final_kernel,
        grid=(b, s // tm),
        in_specs=[row, row, pl.BlockSpec((1, 1, d), lambda i, j: (i, 0, 0)),
                  pl.BlockSpec((1, d), lambda i, j: (0, 0))],
        out_specs=row,
        out_shape=jax.ShapeDtypeStruct((b, s, d), F32),
        compiler_params=_cparams("parallel", "parallel"),
        name="final_norm",
    )(h1, peer_out, g2, gain.reshape(1, d))


def kernel(x, c, ctx, c_ctx, w_mod, b_mod, norm1, w_in, ssd_conv_w, ssd_conv_b, ssd_a_log, ssd_dt_bias, ssd_d, ssd_norm, hy_short_w, hy_short_b, hy_w1, hy_b1, hy_w2, hy_b2, hy_w3, hy_sin_freq, hy_filt_bias, hy_norm, w_out, norm2, peer_wq, peer_subkeys, peer_u, peer_v, final_norm):
    bsz, seq, d = x.shape
    ctx_len = ctx.shape[1]
    w_ssd = SSD_HEADS * HEAD_DIM
    gn = SSD_GROUPS * SSD_STATE
    conv_dim = w_ssd + 2 * gn
    off_dt = w_ssd + conv_dim
    off_hy = off_dt + 2 * SSD_HEADS
    w_hy = (w_in.shape[-1] - off_hy) // 3
    assert bsz == 2 and seq // GRID_W == 2 * FFT_N2 and w_hy == w_ssd

    i = 0
    c_rows = jnp.zeros((8, d), F32).at[:bsz].set(c).at[bsz].set(c_ctx)
    mod = modulation(c_rows, w_mod[i], b_mod[i])
    mod_l = mod[:bsz].reshape(bsz, 1, 6, d)
    mod_c = jnp.broadcast_to(mod[bsz].reshape(1, 1, 6, d), (bsz, 1, 6, d))
    sh1_l, sc1_l, g1_l, sh2_l, sc2_l, g2_l = [mod_l[:, :, j] for j in range(6)]
    sh1_c, sc1_c = mod_c[:, :, 0], mod_c[:, :, 1]

    wi = w_in[i].astype(BF16)
    wz, wx, wh = wi[:, :w_ssd], wi[:, w_ssd:off_dt], wi[:, off_hy:]
    wd = jnp.zeros((d, LANE), BF16).at[:, :2 * SSD_HEADS].set(wi[:, off_dt:off_hy])
    z_l, xbc_l, dt_l, hy_l = in_projection(x, sh1_l, sc1_l, norm1[i], wz, wx, wd, wh, tm=512)
    _, xbc_c, dt_c, _ = in_projection(ctx, sh1_c, sc1_c, norm1[i], wz, wx, wd, wh, tm=ctx_len)

    xa_l = dwconv(xbc_l, ssd_conv_w[i], ssd_conv_b[i], 0, conv_dim, True, BF16, tm=512)
    xa_c = dwconv(xbc_c, ssd_conv_w[i], ssd_conv_b[i], 0, conv_dim, True, BF16, tm=ctx_len)
    pad_row = lambda v: jnp.zeros((1, LANE), F32).at[0, :v.shape[0]].set(v)
    bias_row = pad_row(ssd_dt_bias[i].reshape(-1))
    a_row = pad_row(-jnp.exp(ssd_a_log[i].reshape(-1)))
    dskip = jnp.repeat(ssd_d[i], HEAD_DIM).reshape(1, w_ssd)
    h0 = jnp.zeros((bsz, SSD_GROUPS, SSD_STATE, SSD_HPG * HEAD_DIM), F32)
    _, s_f = ssd_scan(xa_c, dt_c, bias_row, a_row, dskip, h0, rev=False)
    _, s_b = ssd_scan(xa_c, dt_c, bias_row, a_row, dskip, h0, rev=True)
    y_f, _ = ssd_scan(xa_l, dt_l, bias_row, a_row, dskip, s_f, rev=False)
    y_b, _ = ssd_scan(xa_l, dt_l, bias_row, a_row, dskip, s_b, rev=True)

    v = dwconv(hy_l, hy_short_w[i][:, :w_hy], hy_short_b[i][:w_hy], 0, w_hy, False, F32, tm=512)
    x12 = dwconv(hy_l, hy_short_w[i][:, w_hy:], hy_short_b[i][w_hy:], w_hy, 2 * w_hy, False, BF16, tm=512)
    kfull = hyena_filters(seq, hy_w1[i], hy_b1[i], hy_w2[i], hy_b2[i], hy_w3[i], hy_sin_freq[i], w_hy)
    tab_n = _dft_tables(seq, colmajor=False)
    tab_c = _dft_tables(seq, colmajor=True)
    kspec = fft_inner_spectrum(fft_filter_outer(kfull, tab_n[2]), tab_n[3])
    y1 = long_conv(v, kspec, 0, tab_n, colmajor=False)
    zz = hyena_gate(y1, v, x12, hy_filt_bias[i, 0])
    y2 = long_conv(zz, kspec, w_hy, tab_c, colmajor=True)

    h1, f_mod, scores_t = mixer_output(
        x, y_f, y_b, z_l, y2, zz, x12, hy_filt_bias[i, 1], ssd_norm[i], hy_norm[i],
        w_out[i].astype(BF16), g1_l, norm2[i], sh2_l, sc2_l, peer_wq[i].astype(BF16), peer_subkeys[i])

    n_tok = bsz * seq
    idx, gate = peer_topk(scores_t)
    act = peer_activations(idx, f_mod.reshape(n_tok, d // LANE, LANE), gate, _pack_table(peer_u[i]))
    p_out = peer_outputs(idx, act, _pack_table(peer_v[i]))
    return final_norm_residual(h1, p_out.reshape(bsz, seq, d), g2_l, final_norm)
```

```python
import functools
import math

import jax
import jax.numpy as jnp
from jax import lax
from jax.experimental import pallas as pl
from jax.experimental.pallas import tpu as pltpu

F32 = jnp.float32
BF16 = jnp.bfloat16
HIGHEST = lax.Precision.HIGHEST

LANE = 128
VMEM_LIMIT = 48 * 1024 * 1024

RMS_EPS = 1e-6
GRID_W = 64
HEAD_DIM = 64
SSD_GROUPS = 4
SSD_HPG = 4
SSD_HEADS = SSD_GROUPS * SSD_HPG
SSD_STATE = 128
SSD_CHUNK = 128
SSD_CONV = 5
HY_SHORT = 3
HY_EMB = 33
HY_BANDS = (HY_EMB - 1) // 2
HY_TARGET = 1e-2
HY_MAX_DECAY = math.log(HY_TARGET) / 0.3
HY_MIN_DECAY = math.log(HY_TARGET) / 1.5
PEER_HEADS = 8
PEER_KEYS = 128
PEER_TOPK = 16
PEER_HALF = 128
FFT_N2 = 128
HALO = 16


def _cparams(*sem):
    return pltpu.CompilerParams(dimension_semantics=sem, vmem_limit_bytes=VMEM_LIMIT)


def _silu(x):
    return x * jax.nn.sigmoid(x)


def _mod_kernel(c_ref, w_ref, b_ref, o_ref):
    s = _silu(c_ref[...])
    o_ref[...] = jnp.dot(s, w_ref[...], precision=HIGHEST, preferred_element_type=F32) + b_ref[...]


def modulation(c_rows, w_mod, b_mod):
    d, n = w_mod.shape
    tn = 512
    return pl.pallas_call(
        _mod_kernel,
        grid=(n // tn,),
        in_specs=[pl.BlockSpec((8, d), lambda j: (0, 0)),
                  pl.BlockSpec((d, tn), lambda j: (0, j)),
                  pl.BlockSpec((1, tn), lambda j: (0, j))],
        out_specs=pl.BlockSpec((8, tn), lambda j: (0, j)),
        out_shape=jax.ShapeDtypeStruct((8, n), F32),
        compiler_params=_cparams("arbitrary"),
        name="modulation",
    )(c_rows, w_mod, b_mod.reshape(1, n))


def _inproj_kernel(x_ref, sh_ref, sc_ref, g_ref, wz_ref, wx_ref, wd_ref, wh_ref,
                   z_ref, xbc_ref, dt_ref, hy_ref):
    x = x_ref[0]
    xn = x * lax.rsqrt(jnp.mean(x * x, axis=-1, keepdims=True) + RMS_EPS) * g_ref[...]
    xm = (xn * (1.0 + sc_ref[0]) + sh_ref[0]).astype(BF16)
    z_ref[0] = jnp.dot(xm, wz_ref[...], preferred_element_type=F32).astype(z_ref.dtype)
    xbc_ref[0] = jnp.dot(xm, wx_ref[...], preferred_element_type=F32).astype(xbc_ref.dtype)
    dt_ref[0] = jnp.dot(xm, wd_ref[...], preferred_element_type=F32)
    hy_ref[0] = jnp.dot(xm, wh_ref[...], preferred_element_type=F32).astype(hy_ref.dtype)


def in_projection(x, shift, scale, gain, wz, wx, wd, wh, tm):
    b, s, d = x.shape
    const = lambda w: pl.BlockSpec(w.shape, lambda i, j: (0, 0), pipeline_mode=pl.Buffered(1))
    row = lambda n: pl.BlockSpec((1, tm, n), lambda i, j: (i, j, 0))
    vec = pl.BlockSpec((1, 1, d), lambda i, j: (i, 0, 0))
    return pl.pallas_call(
        _inproj_kernel,
        grid=(b, s // tm),
        in_specs=[row(d), vec, vec, pl.BlockSpec((1, d), lambda i, j: (0, 0)),
                  const(wz), const(wx), const(wd), const(wh)],
        out_specs=[row(wz.shape[1]), row(wx.shape[1]), row(wd.shape[1]), row(wh.shape[1])],
        out_shape=[jax.ShapeDtypeStruct((b, s, wz.shape[1]), BF16),
                   jax.ShapeDtypeStruct((b, s, wx.shape[1]), BF16),
                   jax.ShapeDtypeStruct((b, s, wd.shape[1]), F32),
                   jax.ShapeDtypeStruct((b, s, wh.shape[1]), BF16)],
        compiler_params=_cparams("parallel", "parallel"),
        name="in_projection",
    )(x, shift, scale, gain.reshape(1, d), wz, wx, wd, wh)


def _dwconv_kernel(prev_ref, main_ref, next_ref, w_ref, b_ref, o_ref, ext_ref, *, taps, act, tm, nt):
    i = pl.program_id(1)
    ext_ref[pl.ds(0, HALO), :] = jnp.where(i > 0, prev_ref[0].astype(F32), 0.0)
    ext_ref[pl.ds(HALO, tm), :] = main_ref[0].astype(F32)
    ext_ref[pl.ds(HALO + tm, HALO), :] = jnp.where(i < nt - 1, next_ref[0].astype(F32), 0.0)
    acc = jnp.broadcast_to(b_ref[...], (tm, b_ref.shape[1]))
    for k in range(taps):
        acc = acc + w_ref[pl.ds(k, 1), :] * ext_ref[pl.ds(HALO - taps // 2 + k, tm), :]
    if act:
        acc = _silu(acc)
    o_ref[0] = acc.astype(o_ref.dtype)


def dwconv(u, w, bias, col0, ncols, act, out_dtype, tm, cb=512):
    b, s, _ = u.shape
    taps = w.shape[0]
    nt = s // tm
    hb = tm // HALO
    c0 = col0 // cb
    w8 = jnp.zeros((8, ncols), F32).at[:taps].set(w)
    kern = functools.partial(_dwconv_kernel, taps=taps, act=act, tm=tm, nt=nt)
    return pl.pallas_call(
        kern,
        grid=(b, nt, ncols // cb),
        in_specs=[pl.BlockSpec((1, HALO, cb), lambda bi, i, j: (bi, jnp.maximum(i * hb - 1, 0), j + c0)),
                  pl.BlockSpec((1, tm, cb), lambda bi, i, j: (bi, i, j + c0)),
                  pl.BlockSpec((1, HALO, cb), lambda bi, i, j: (bi, jnp.minimum((i + 1) * hb, s // HALO - 1), j + c0)),
                  pl.BlockSpec((8, cb), lambda bi, i, j: (0, j)),
                  pl.BlockSpec((1, cb), lambda bi, i, j: (0, j))],
        out_specs=pl.BlockSpec((1, tm, cb), lambda bi, i, j: (bi, i, j)),
        out_shape=jax.ShapeDtypeStruct((b, s, ncols), out_dtype),
        scratch_shapes=[pltpu.VMEM((tm + 2 * HALO, cb), F32)],
        compiler_params=_cparams("parallel", "parallel", "parallel"),
        name="dwconv",
    )(u, u, u, w8, bias.reshape(1, ncols))


def _ssd_kernel(x_ref, b_ref, c_ref, dt_ref, bias_ref, a_ref, dskip_ref, h0_ref,
                y_ref, st_ref, *, rev, col0):
    q = SSD_CHUNK
    gw = SSD_HPG * HEAD_DIM

    @pl.when(pl.program_id(1) == 0)
    def _():
        st_ref[...] = h0_ref[...]

    dtp = jax.nn.softplus(dt_ref[0] + bias_ref[...])
    a = dtp * a_ref[...]
    li = lax.broadcasted_iota(jnp.int32, (q, q), 0)
    si = lax.broadcasted_iota(jnp.int32, (q, q), 1)
    tri = (si <= li).astype(F32)
    cum = jnp.dot(tri, a, precision=HIGHEST, preferred_element_type=F32)
    total = cum[q - 1:q, :]
    if rev:
        r = a - cum
        keep = si >= li
        d_out = jnp.exp(total + r)
        d_st = jnp.exp(-r)
    else:
        r = cum
        keep = li >= si
        d_out = jnp.exp(r)
        d_st = jnp.exp(total - r)
    r_t = jnp.transpose(r)
    d_tot = jnp.exp(total)
    lane_head = lax.broadcasted_iota(jnp.int32, (1, gw), 1) // HEAD_DIM

    x = x_ref[0].astype(F32)
    for g in range(SSD_GROUPS):
        cg = c_ref[0, :, g * SSD_STATE:(g + 1) * SSD_STATE]
        bg = b_ref[0, :, g * SSD_STATE:(g + 1) * SSD_STATE]
        cb = lax.dot_general(cg, bg, (((1,), (1,)), ((), ())), preferred_element_type=F32)
        xg = x[:, g * gw:(g + 1) * gw]
        dt_g = jnp.zeros((q, gw), F32)
        dout_g = jnp.zeros((q, gw), F32)
        dst_g = jnp.zeros((q, gw), F32)
        dtot_g = jnp.zeros((1, gw), F32)
        for hh in range(SSD_HPG):
            hc = col0 + g * SSD_HPG + hh
            m = lane_head == hh
            dt_g = jnp.where(m, dtp[:, hc:hc + 1], dt_g)
            dout_g = jnp.where(m, d_out[:, hc:hc + 1], dout_g)
            dst_g = jnp.where(m, d_st[:, hc:hc + 1], dst_g)
            dtot_g = jnp.where(m, d_tot[:, hc:hc + 1], dtot_g)
        xdt = xg * dt_g
        st_old = st_ref[0, g]
        y_g = jnp.dot(cg, st_old.astype(BF16), preferred_element_type=F32) * dout_g
        for hh in range(SSD_HPG):
            hc = col0 + g * SSD_HPG + hh
            diff = r[:, hc:hc + 1] - r_t[hc:hc + 1, :]
            lm = jnp.where(keep, jnp.exp(jnp.minimum(diff, 0.0)), 0.0)
            mh = (cb * lm).astype(BF16)
            xh = jnp.where(lane_head == hh, xdt, 0.0).astype(BF16)
            y_g = y_g + jnp.dot(mh, xh, preferred_element_type=F32)
        if not rev:
            y_g = y_g + xg * dskip_ref[:, g * gw:(g + 1) * gw]
        y_ref[0, :, g * gw:(g + 1) * gw] = y_g.astype(y_ref.dtype)
        upd = lax.dot_general(bg, (xdt * dst_g).astype(BF16), (((0,), (0,)), ((), ())),
                              preferred_element_type=F32)
        st_ref[0, g] = st_old * dtot_g + upd


def ssd_scan(xbc, dt_raw, dt_bias_row, a_row, dskip_row, h0, rev):
    b, s, _ = xbc.shape
    w = SSD_HEADS * HEAD_DIM
    gn = SSD_GROUPS * SSD_STATE
    nc = s // SSD_CHUNK
    cidx = (lambda c: nc - 1 - c) if rev else (lambda c: c)
    kern = functools.partial(_ssd_kernel, rev=rev, col0=SSD_HEADS if rev else 0)
    st_spec = pl.BlockSpec((1, SSD_GROUPS, SSD_STATE, SSD_HPG * HEAD_DIM), lambda bi, c: (bi, 0, 0, 0))
    return pl.pallas_call(
        kern,
        grid=(b, nc),
        in_specs=[pl.BlockSpec((1, SSD_CHUNK, w), lambda bi, c: (bi, cidx(c), 0)),
                  pl.BlockSpec((1, SSD_CHUNK, gn), lambda bi, c: (bi, cidx(c), w // gn)),
                  pl.BlockSpec((1, SSD_CHUNK, gn), lambda bi, c: (bi, cidx(c), w // gn + 1)),
                  pl.BlockSpec((1, SSD_CHUNK, LANE), lambda bi, c: (bi, cidx(c), 0)),
                  pl.BlockSpec((1, LANE), lambda bi, c: (0, 0)),
                  pl.BlockSpec((1, LANE), lambda bi, c: (0, 0)),
                  pl.BlockSpec((1, w), lambda bi, c: (0, 0)),
                  st_spec],
        out_specs=[pl.BlockSpec((1, SSD_CHUNK, w), lambda bi, c: (bi, cidx(c), 0)), st_spec],
        out_shape=[jax.ShapeDtypeStruct((b, s, w), BF16),
                   jax.ShapeDtypeStruct(h0.shape, F32)],
        compiler_params=_cparams("parallel", "arbitrary"),
        name="ssd_scan_rev" if rev else "ssd_scan_fwd",
    )(xbc, xbc, xbc, dt_raw, dt_bias_row, a_row, dskip_row, h0)


def _filter_kernel(z_ref, w1_ref, b1_ref, w2_ref, b2_ref, w3a_ref, w3b_ref, f0_ref, f1_ref, dl_ref, o_ref,
                   *, seqlen):
    z = z_ref[...]
    h = jnp.sin(f0_ref[...] * (jnp.dot(z, w1_ref[...], precision=HIGHEST, preferred_element_type=F32) + b1_ref[...]))
    h = jnp.sin(f1_ref[...] * (jnp.dot(h, w2_ref[...], precision=HIGHEST, preferred_element_type=F32) + b2_ref[...]))
    h_hi = h.astype(BF16)
    h_lo = (h - h_hi.astype(F32)).astype(BF16)
    tl = z.shape[0]
    width = w3a_ref.shape[1]
    n = pl.program_id(0) * tl + lax.broadcasted_iota(jnp.int32, (tl, 1), 0)
    decay = jnp.where(n == seqlen, 0.0, jnp.exp(-z[:, 0:1] * dl_ref[...]))
    for c, w3_ref in enumerate((w3a_ref, w3b_ref)):
        w3 = w3_ref[...]
        w_hi = w3.astype(BF16)
        w_lo = (w3 - w_hi.astype(F32)).astype(BF16)
        y = (jnp.dot(h_hi, w_hi, preferred_element_type=F32) + jnp.dot(h_hi, w_lo, preferred_element_type=F32)
             + jnp.dot(h_lo, w_hi, preferred_element_type=F32))
        o_ref[:, c * width:(c + 1) * width] = y * decay


def hyena_filters(seqlen, w1, b1, w2, b2, w3, sin_freq, width):
    t = jnp.linspace(0.0, 1.0, seqlen, dtype=F32)[:, None]
    w_ang = 2.0 * math.pi * jnp.arange(seqlen, dtype=F32) / seqlen
    bands = jnp.linspace(1e-4, HY_BANDS - 1, HY_BANDS, dtype=F32)
    ang = w_ang[:, None] * bands[None, :]
    zpos = jnp.concatenate([t, jnp.cos(ang), -jnp.sin(ang)], axis=-1)
    zboth = jnp.concatenate([zpos, jnp.zeros_like(zpos[:1]), zpos[:0:-1]], axis=0)
    zpad = jnp.zeros((2 * seqlen, LANE), F32).at[:, :HY_EMB].set(zboth)
    hid = w1.shape[1]
    w1p = jnp.zeros((LANE, hid), F32).at[:HY_EMB].set(w1)
    deltas = jnp.abs(jnp.linspace(HY_MIN_DECAY, HY_MAX_DECAY, width, dtype=F32))
    dl = deltas.reshape(1, width)
    tl = 512
    nfwd = seqlen // tl
    full = lambda a: pl.BlockSpec(a.shape, lambda i: (0, 0))
    vec = pl.BlockSpec((1, hid), lambda i: (0, 0))
    w3_spec = lambda c: pl.BlockSpec((hid, width), lambda i: (0, 2 * c + i // nfwd))
    return pl.pallas_call(
        functools.partial(_filter_kernel, seqlen=seqlen),
        grid=(2 * seqlen // tl,),
        in_specs=[pl.BlockSpec((tl, LANE), lambda i: (i, 0)),
                  full(w1p), vec, full(w2), vec, w3_spec(0), w3_spec(1), vec, vec,
                  pl.BlockSpec((1, width), lambda i: (0, 0))],
        out_specs=pl.BlockSpec((tl, 2 * width), lambda i: (i, 0)),
        out_shape=jax.ShapeDtypeStruct((2 * seqlen, 2 * width), F32),
        compiler_params=_cparams("parallel"),
        name="hyena_filters",
    )(zpad, w1p, b1.reshape(1, hid), w2, b2.reshape(1, hid), w3, w3,
      sin_freq[0].reshape(1, hid), sin_freq[1].reshape(1, hid), dl)


def _dft_tables(seqlen, colmajor):
    n = 2 * seqlen
    n2 = FFT_N2
    n1 = n // n2
    h1 = n1 // 2
    k1 = jnp.arange(n1, dtype=jnp.int32)
    m1 = jnp.arange(h1, dtype=jnp.int32)
    if colmajor:
        m1 = 2 * (m1 % GRID_W) + m1 // GRID_W
    ph = (k1[:, None] * m1[None, :] * n2) % n
    ang = (-2.0 * math.pi / n) * ph.astype(F32)
    wr, wi = jnp.cos(ang), jnp.sin(ang)
    w_fwd = jnp.concatenate([jnp.concatenate([wr, -wi], 1), jnp.concatenate([wi, wr], 1)], 0)
    w_inv = jnp.concatenate([jnp.concatenate([wr.T, wi.T], 1), jnp.concatenate([-wi.T, wr.T], 1)], 0) / n
    mf = jnp.arange(n1, dtype=jnp.int32)
    phf = (k1[:, None] * mf[None, :] * n2) % n
    angf = (-2.0 * math.pi / n) * phf.astype(F32)
    w_flt = jnp.concatenate([jnp.cos(angf), jnp.sin(angf)], 0)
    a2 = jnp.arange(n2, dtype=jnp.int32)
    ph2 = (a2[None, :, None] * a2[None, None, :] * n1 + k1[:, None, None] * a2[None, None, :]) % n
    ang2 = (-2.0 * math.pi / n) * ph2.astype(F32)
    c2, s2 = jnp.cos(ang2), jnp.sin(ang2)
    w2_fwd = jnp.concatenate([jnp.concatenate([c2, -s2], 2), jnp.concatenate([s2, c2], 2)], 1)
    c2t, s2t = jnp.swapaxes(c2, 1, 2), jnp.swapaxes(s2, 1, 2)
    w2_inv = jnp.concatenate([jnp.concatenate([c2t, s2t], 2), jnp.concatenate([-s2t, c2t], 2)], 1)
    return (w_fwd.astype(BF16), w_inv.astype(BF16), w_flt.astype(BF16),
            w2_fwd.astype(BF16), w2_inv.astype(BF16))


def _fft_outer_fwd_kernel(x_ref, w_ref, o_ref, *, colmajor, nb):
    n1 = o_ref.shape[0]
    for j in range(8):
        parts = []
        for bi in range(nb):
            if colmajor:
                parts.append(x_ref[bi, 0, pl.ds(j * GRID_W, GRID_W), :].astype(F32))
                parts.append(x_ref[bi, 1, pl.ds(j * GRID_W, GRID_W), :].astype(F32))
            else:
                parts.append(x_ref[bi, :, j, :])
        xs = jnp.concatenate(parts, axis=0).astype(BF16)
        a = jnp.dot(w_ref[...], xs, preferred_element_type=F32)
        o_ref[:, 0, j, :] = a[:n1]
        o_ref[:, 1, j, :] = a[n1:]


def fft_outer_fwd(x, w, colmajor, cblk=512):
    b, seqlen, c = x.shape
    n2 = FFT_N2
    n1 = 2 * seqlen // n2
    if colmajor:
        xv = x.reshape(b, 2, seqlen // 2, c)
        in_spec = pl.BlockSpec((b, 2, 8 * GRID_W, cblk), lambda g, j: (0, 0, g, j))
    else:
        xv = x.reshape(b, n1 // 2, n2, c)
        in_spec = pl.BlockSpec((b, n1 // 2, 8, cblk), lambda g, j: (0, 0, g, j))
    kern = functools.partial(_fft_outer_fwd_kernel, colmajor=colmajor, nb=b)
    return pl.pallas_call(
        kern,
        grid=(n2 // 8, c // cblk),
        in_specs=[in_spec, pl.BlockSpec(w.shape, lambda g, j: (0, 0))],
        out_specs=pl.BlockSpec((n1, 2, 8, cblk), lambda g, j: (0, 0, g, j)),
        out_shape=jax.ShapeDtypeStruct((n1, 2, n2, c), F32),
        compiler_params=_cparams("parallel", "parallel"),
        name="fft_outer_fwd_cm" if colmajor else "fft_outer_fwd",
    )(xv, w)


def _fft_filter_outer_kernel(k_ref, w_ref, o_ref):
    n1 = o_ref.shape[0]
    for j in range(8):
        a = jnp.dot(w_ref[...], k_ref[:, j, :].astype(BF16), preferred_element_type=F32)
        o_ref[:, 0, j, :] = a[:n1]
        o_ref[:, 1, j, :] = a[n1:]


def fft_filter_outer(k, w, cblk=512):
    n, c = k.shape
    n2 = FFT_N2
    n1 = n // n2
    return pl.pallas_call(
        _fft_filter_outer_kernel,
        grid=(n2 // 8, c // cblk),
        in_specs=[pl.BlockSpec((n1, 8, cblk), lambda g, j: (0, g, j)),
                  pl.BlockSpec(w.shape, lambda g, j: (0, 0))],
        out_specs=pl.BlockSpec((n1, 2, 8, cblk), lambda g, j: (0, 0, g, j)),
        out_shape=jax.ShapeDtypeStruct((n1, 2, n2, c), F32),
        compiler_params=_cparams("parallel", "parallel"),
        name="fft_filter_outer",
    )(k.reshape(n1, n2, c), w)


def _fft_inner_spec_kernel(a_ref, wf_ref, o_ref):
    n2 = FFT_N2
    cb = a_ref.shape[-1]
    for j in range(8):
        xs = a_ref[j].reshape(2 * n2, cb).astype(BF16)
        o_ref[j] = jnp.dot(wf_ref[j], xs, preferred_element_type=F32).reshape(2, n2, cb)


def fft_inner_spectrum(a, w2_fwd, cblk=512):
    n1, _, n2, c = a.shape
    blk = pl.BlockSpec((8, 2, n2, cblk), lambda g, j: (g, 0, 0, j))
    return pl.pallas_call(
        _fft_inner_spec_kernel,
        grid=(n1 // 8, c // cblk),
        in_specs=[blk, pl.BlockSpec((8, 2 * n2, 2 * n2), lambda g, j: (g, 0, 0))],
        out_specs=blk,
        out_shape=jax.ShapeDtypeStruct(a.shape, F32),
        compiler_params=_cparams("parallel", "arbitrary"),
        name="fft_inner_spectrum",
    )(a, w2_fwd)


def _fft_inner_conv_kernel(a_ref, k_ref, wf_ref, wi_ref, o_ref):
    n2 = FFT_N2
    cb = a_ref.shape[-1]
    for j in range(8):
        xs = a_ref[j].reshape(2 * n2, cb).astype(BF16)
        s = jnp.dot(wf_ref[j], xs, preferred_element_type=F32)
        sr, si = s[:n2], s[n2:]
        kr, ki = k_ref[j, 0], k_ref[j, 1]
        y = jnp.concatenate([sr * kr - si * ki, sr * ki + si * kr], axis=0).astype(BF16)
        z = jnp.dot(wi_ref[j], y, preferred_element_type=F32)
        o_ref[:, 0, j, :] = z[:n2]
        o_ref[:, 1, j, :] = z[n2:]


def fft_inner_conv(a, kspec, kcol0, w2_fwd, w2_inv, cblk=512):
    n1, _, n2, c = a.shape
    kc = kcol0 // cblk
    wspec = pl.BlockSpec((8, 2 * n2, 2 * n2), lambda g, j: (g, 0, 0))
    return pl.pallas_call(
        _fft_inner_conv_kernel,
        grid=(n1 // 8, c // cblk),
        in_specs=[pl.BlockSpec((8, 2, n2, cblk), lambda g, j: (g, 0, 0, j)),
                  pl.BlockSpec((8, 2, n2, cblk), lambda g, j: (g, 0, 0, j + kc)),
                  wspec, wspec],
        out_specs=pl.BlockSpec((n2, 2, 8, cblk), lambda g, j: (0, 0, g, j)),
        out_shape=jax.ShapeDtypeStruct((n2, 2, n1, c), F32),
        compiler_params=_cparams("parallel", "arbitrary"),
        name="fft_inner_conv",
    )(a, kspec, w2_fwd, w2_inv)


def _fft_outer_inv_kernel(b_ref, w_ref, o_ref, *, colmajor, nb):
    cb = b_ref.shape[-1]
    n1 = b_ref.shape[2]
    h1 = n1 // 2
    for j in range(8):
        xs = b_ref[j].reshape(2 * n1, cb).astype(BF16)
        y = jnp.dot(w_ref[...], xs, preferred_element_type=F32)
        for bi in range(nb):
            yb = y[bi * h1:(bi + 1) * h1]
            if colmajor:
                o_ref[bi, 0, pl.ds(j * GRID_W, GRID_W), :] = yb[:GRID_W]
                o_ref[bi, 1, pl.ds(j * GRID_W, GRID_W), :] = yb[GRID_W:]
            else:
                o_ref[bi, :, j, :] = yb


def fft_outer_inv(bsp, w, nb, colmajor, cblk=512):
    n2, _, n1, c = bsp.shape
    seqlen = n1 * n2 // 2
    if colmajor:
        out_spec = pl.BlockSpec((nb, 2, 8 * GRID_W, cblk), lambda g, j: (0, 0, g, j))
        out_shape = jax.ShapeDtypeStruct((nb, 2, seqlen // 2, c), F32)
    else:
        out_spec = pl.BlockSpec((nb, n1 // 2, 8, cblk), lambda g, j: (0, 0, g, j))
        out_shape = jax.ShapeDtypeStruct((nb, n1 // 2, n2, c), F32)
    kern = functools.partial(_fft_outer_inv_kernel, colmajor=colmajor, nb=nb)
    y = pl.pallas_call(
        kern,
        grid=(n2 // 8, c // cblk),
        in_specs=[pl.BlockSpec((8, 2, n1, cblk), lambda g, j: (g, 0, 0, j)),
                  pl.BlockSpec(w.shape, lambda g, j: (0, 0))],
        out_specs=out_spec,
        out_shape=out_shape,
        compiler_params=_cparams("parallel", "parallel"),
        name="fft_outer_inv_cm" if colmajor else "fft_outer_inv",
    )(bsp, w)
    return y.reshape(nb, seqlen, c)


def long_conv(u, kspec, kcol0, tables, colmajor):
    w_fwd, w_inv, _, w2_fwd, w2_inv = tables
    a = fft_outer_fwd(u, w_fwd, colmajor)
    bsp = fft_inner_conv(a, kspec, kcol0, w2_fwd, w2_inv)
    return fft_outer_inv(bsp, w_inv, u.shape[0], colmajor)


def _hy_gate_kernel(y_ref, v_ref, x1_ref, fb_ref, o_ref):
    o_ref[0] = x1_ref[0].astype(F32) * (y_ref[0] + v_ref[0] * fb_ref[...])


def hyena_gate(y, v, x12, fbias, tm=512, cb=512):
    b, s, c = y.shape
    blk = lambda: pl.BlockSpec((1, tm, cb), lambda bi, i, j: (bi, i, j))
    return pl.pallas_call(
        _hy_gate_kernel,
        grid=(b, s // tm, c // cb),
        in_specs=[blk(), blk(), blk(), pl.BlockSpec((1, cb), lambda bi, i, j: (0, j))],
        out_specs=blk(),
        out_shape=jax.ShapeDtypeStruct((b, s, c), F32),
        compiler_params=_cparams("parallel", "parallel", "parallel"),
        name="hyena_gate",
    )(y, v, x12, fbias.reshape(1, c))


def _mix_kernel(x_ref, yf_ref, yb_ref, z_ref, y2_ref, zz_ref, x12_ref,
                fb_ref, gs_ref, gh_ref, mh_ref, wo_ref, g1_ref, n2_ref, sh2_ref, sc2_ref,
                wq_ref, sk_ref, h1_ref, f_ref, st_ref):
    w = yf_ref.shape[-1]
    gw = w // SSD_GROUPS
    ys = (yf_ref[0].astype(F32) + yb_ref[0].astype(F32)) * _silu(z_ref[0].astype(F32))
    parts = []
    for g in range(SSD_GROUPS):
        blk = ys[:, g * gw:(g + 1) * gw]
        ms = jnp.mean(blk * blk, axis=-1, keepdims=True)
        parts.append(blk * lax.rsqrt(ms + RMS_EPS))
    o_s = jnp.concatenate(parts, axis=1) * gs_ref[...]
    yh = x12_ref[0].astype(F32) * (y2_ref[0] + zz_ref[0] * fb_ref[...])
    parts = []
    for c in range(w // LANE):
        blk = yh[:, c * LANE:(c + 1) * LANE]
        sq = blk * blk
        hi = sq.astype(BF16)
        lo = (sq - hi.astype(F32)).astype(BF16)
        ms = (jnp.dot(hi, mh_ref[...], preferred_element_type=F32)
              + jnp.dot(lo, mh_ref[...], preferred_element_type=F32)) * (1.0 / HEAD_DIM)
        parts.append(blk * lax.rsqrt(ms + RMS_EPS))
    o_h = jnp.concatenate(parts, axis=1) * gh_ref[...]
    mix = (jnp.dot(o_s.astype(BF16), wo_ref[pl.ds(0, w), :], preferred_element_type=F32)
           + jnp.dot(o_h.astype(BF16), wo_ref[pl.ds(w, w), :], preferred_element_type=F32))
    h1 = x_ref[0] + g1_ref[0] * mix
    h1_ref[0] = h1
    hn = h1 * lax.rsqrt(jnp.mean(h1 * h1, axis=-1, keepdims=True) + RMS_EPS) * n2_ref[...]
    f = hn * (1.0 + sc2_ref[0]) + sh2_ref[0]
    f_ref[0] = f
    qv =jnp.dot(f.astype(BF16), wq_ref[...], preferred_element_type=F32).astype(BF16)
    for blk in range(2 * PEER_HEADS):
        qb = qv[:, blk * PEER_HALF:(blk + 1) * PEER_HALF]
        st_ref[blk] = lax.dot_general(sk_ref[blk % 2], qb, (((1,), (1,)), ((), ())),
                                      preferred_element_type=F32)


def mixer_output(x, y_f, y_b, z, y2, zz, x12, fbias, g_ssd, g_hy, w_out, g1, norm2, sh2, sc2, wq, subkeys, tm=256):
    b, s, d = x.shape
    w = y_f.shape[-1]
    nblk = 2 * PEER_HEADS
    lane = jnp.arange(LANE)
    m_h = (lane[:, None] // HEAD_DIM == lane[None, :] // HEAD_DIM).astype(BF16)
    row = lambda n, c0=0: pl.BlockSpec((1, tm, n), lambda i, j: (i, j, c0))
    vec = lambda n: pl.BlockSpec((1, 1, n), lambda i, j: (i, 0, 0))
    const = lambda a: pl.BlockSpec(a.shape, lambda i, j: (0,) * a.ndim, pipeline_mode=pl.Buffered(1))
    sk = subkeys.astype(BF16)
    return pl.pallas_call(
        _mix_kernel,
        grid=(b, s // tm),
        in_specs=[row(d), row(w), row(w), row(w), row(w), row(w), row(w, 1),
                  pl.BlockSpec((1, w), lambda i, j: (0, 0)),
                  pl.BlockSpec((1, w), lambda i, j: (0, 0)),
                  pl.BlockSpec((1, w), lambda i, j: (0, 0)),
                  const(m_h), const(w_out), vec(d),
                  pl.BlockSpec((1, d), lambda i, j: (0, 0)), vec(d), vec(d),
                  const(wq), const(sk)],
        out_specs=[row(d), row(d),
                   pl.BlockSpec((nblk, PEER_KEYS, tm), lambda i, j: (0, 0, i * (s // tm) + j))],
        out_shape=[jax.ShapeDtypeStruct((b, s, d), F32),
                   jax.ShapeDtypeStruct((b, s, d), F32),
                   jax.ShapeDtypeStruct((nblk, PEER_KEYS, b * s), F32)],
        compiler_params=_cparams("parallel", "parallel"),
        name="mixer_output",
    )(x, y_f, y_b, z, y2, zz, x12, fbias.reshape(1, w), g_ssd.reshape(1, w), g_hy.reshape(1, w),
      m_h, w_out, g1, norm2.reshape(1, d), sh2, sc2, wq, sk)


def _topk_rows(x, riota, k):
    t = x.shape[1]
    slot = lax.broadcasted_iota(jnp.int32, (k, t), 0)
    vals = jnp.zeros((k, t), F32)
    idxs = jnp.zeros((k, t), F32)
    big = jnp.float32(1e9)
    for it in range(k):
        m = jnp.max(x, axis=0, keepdims=True)
        sel = jnp.min(jnp.where(x == m, riota, big), axis=0, keepdims=True)
        vals = jnp.where(slot == it, m, vals)
        idxs = jnp.where(slot == it, sel, idxs)
        x = jnp.where(riota == sel, -jnp.inf, x)
    return vals, idxs


def _pair_candidates(v1, v2, k):
    assert k == 16
    t = v1.shape[1]
    b16 = lax.broadcasted_iota(jnp.int32, (k, t), 0)
    b8 = lax.broadcasted_iota(jnp.int32, (8, t), 0)
    vals = [v1[0:1, :] + v2]
    ids = [b16.astype(F32)]
    for a in range(1, 8):
        vals.append(jnp.where(b8 < k // (a + 1), v1[a:a + 1, :] + v2[0:8, :], -jnp.inf))
        ids.append((b8 + a * k).astype(F32))
    vals.append(v1[8:16, :] + v2[0:1, :])
    ids.append(((b8 + 8) * k).astype(F32))
    return jnp.concatenate(vals, axis=0), jnp.concatenate(ids, axis=0)


def _topk_kernel(s_ref, idx_ref, gate_ref, idx_t, gate_t):
    k = PEER_TOPK
    t = s_ref.shape[-1]
    key_iota = lax.broadcasted_iota(jnp.int32, (PEER_KEYS, t), 0).astype(F32)
    r16 = lax.broadcasted_iota(jnp.int32, (k, t), 0).astype(F32)
    slot = lax.broadcasted_iota(jnp.int32, (k, t), 0)

    def subkey_topk(h):
        v1, i1 = _topk_rows(s_ref[2 * h], key_iota, k)
        v2, i2 = _topk_rows(s_ref[2 * h + 1], key_iota, k)
        return v1, i1, v2, i2

    def finish(h, v1, i1, v2, i2):
        cand, cid = _pair_candidates(v1, v2, k)
        sc, ci = _topk_rows(cand, cid, k)
        e = jnp.exp(sc - sc[0:1, :])
        ih = jnp.zeros((k, t), F32)
        for j in range(k):
            cj = ci[j:j + 1, :]
            a = jnp.floor(cj * (1.0 / k))
            bb = cj - a * k
            e1 = jnp.sum(jnp.where(r16 == a, i1, 0.0), axis=0, keepdims=True)
            e2 = jnp.sum(jnp.where(r16 == bb, i2, 0.0), axis=0, keepdims=True)
            ih = jnp.where(slot == j, e1 * PEER_KEYS + e2, ih)
        row0 = pl.multiple_of(h * k, k)
        idx_t[pl.ds(row0, k), :] = ih
        gate_t[pl.ds(row0, k), :] = e / jnp.sum(e, axis=0, keepdims=True)

    def body(h, prev):
        finish(h - 1, *prev)
        return subkey_topk(h)

    last = lax.fori_loop(1, PEER_HEADS, body, subkey_topk(0))
    finish(PEER_HEADS - 1, *last)
    idx_ref[...] = jnp.transpose(idx_t[...]).astype(jnp.int32)
    gate_ref[...] = jnp.transpose(gate_t[...])


def peer_topk(scores_t, tt=128):
    nblk, nk, n = scores_t.shape
    hk = PEER_HEADS * PEER_TOPK
    return pl.pallas_call(
        _topk_kernel,
        grid=(n // tt,),
        in_specs=[pl.BlockSpec((nblk, nk, tt), lambda i: (0, 0, i))],
        out_specs=[pl.BlockSpec((tt, hk), lambda i: (i, 0)), pl.BlockSpec((tt, hk), lambda i: (i, 0))],
        out_shape=[jax.ShapeDtypeStruct((n, hk), jnp.int32), jax.ShapeDtypeStruct((n, hk), F32)],
        scratch_shapes=[pltpu.VMEM((hk, tt), F32), pltpu.VMEM((hk, tt), F32)],
        compiler_params=_cparams("parallel"),
        name="peer_topk",
    )(scores_t)


HALF_SUB = 4


def _pack_table(tab):
    e, d = tab.shape
    bits = lax.bitcast_convert_type(tab.astype(BF16), jnp.uint16).astype(jnp.uint32)
    return ((bits[:, d // 2:] << 16) | bits[:, :d // 2]).reshape(e, d // (2 * LANE), LANE)


def _pair_rows(tab_vmem, idx_ref, t, ka, kb):
    row = idx_ref.at[t]
    return jnp.concatenate([tab_vmem[row[ka]], tab_vmem[row[kb]]], axis=0)


def _unpack(wd):
    lo = pltpu.bitcast(wd << 16, F32)
    hi = pltpu.bitcast(wd & jnp.uint32(0xFFFF0000), F32)
    return lo, hi


def _load_table_once(tab_hbm, tab_vmem, sem):
    @pl.when(pl.program_id(0) == 0)
    def _():
        cp = pltpu.make_async_copy(tab_hbm, tab_vmem, sem)
        cp.start()
        cp.wait()


def _peer_act_kernel(idx_ref, f_ref, gate_ref, tab_hbm, act_ref, tab_vmem, part_ref, sem, *, tok):
    _load_table_once(tab_hbm, tab_vmem, sem)
    hk = gate_ref.shape[-1]
    sub = lax.broadcasted_iota(jnp.int32, (8, LANE), 0)
    low = sub < HALF_SUB
    eye = (lax.broadcasted_iota(jnp.int32, (hk, LANE), 0)
           == lax.broadcasted_iota(jnp.int32, (hk, LANE), 1))

    def split_f(t):
        frow = f_ref[pl.ds(t, 1), :]
        chunk = lambda c: frow[:, c * LANE:(c + 1) * LANE]
        s4 = sub & (HALF_SUB - 1)
        pick = lambda c0: jnp.where(s4 == 0, chunk(c0), jnp.where(s4 == 1, chunk(c0 + 1),
                                    jnp.where(s4 == 2, chunk(c0 + 2), chunk(c0 + 3))))
        return pick(0), pick(HALF_SUB)

    def partial_sums(i, carry):
        toks = (2 * i, 2 * i + 1)
        fs = [split_f(t) for t in toks]
        for g in range(hk // 8):
            for t, (flo, fhi) in zip(toks, fs):
                merged = None
                for j in range(HALF_SUB):
                    lo, hi = _unpack(_pair_rows(tab_vmem, idx_ref, t, 8 * g + j, 8 * g + HALF_SUB + j))
                    p = lo * flo + hi * fhi
                    p = p + pltpu.roll(p, 6, axis=0)
                    p = p + pltpu.roll(p, 7, axis=0)
                    if j:
                        p = pltpu.roll(p, j, axis=0)
                    merged = p if merged is None else jnp.where((sub & (HALF_SUB - 1)) == j, p, merged)
                part_ref[t, pl.ds(8 * g, 8), :] = merged
        return carry

    def lane_sums(i, carry):
        rows = []
        for s8 in range(8):
            s = jnp.sum(part_ref[8 * i + s8], axis=1, keepdims=True)
            rows.append(jnp.sum(jnp.where(eye, s, 0.0), axis=0, keepdims=True))
        act_ref[pl.ds(pl.multiple_of(8 * i, 8), 8), :] = jnp.concatenate(rows, axis=0)
        return carry

    lax.fori_loop(0, tok // 2, partial_sums, 0)
    lax.fori_loop(0, tok // 8, lane_sums, 0)
    pre = act_ref[...]
    act_ref[...] = 0.5 * pre * (1.0 + lax.erf(pre * (1.0 / math.sqrt(2.0)))) * gate_ref[...]


def peer_activations(idx, f, gate, table, tok=128):
    n, hk = idx.shape
    kern = functools.partial(_peer_act_kernel, tok=tok)
    return pl.pallas_call(
        kern,
        grid=(n // tok,),
        in_specs=[pl.BlockSpec((tok, hk), lambda i: (i, 0), memory_space=pltpu.SMEM),
                  pl.BlockSpec((tok, f.shape[1]), lambda i: (i, 0)),
                  pl.BlockSpec((tok, hk), lambda i: (i, 0)),
                  pl.BlockSpec(memory_space=pl.ANY)],
        out_specs=pl.BlockSpec((tok, hk), lambda i: (i, 0)),
        out_shape=jax.ShapeDtypeStruct((n, hk), F32),
        scratch_shapes=[pltpu.VMEM(table.shape, table.dtype), pltpu.VMEM((tok, hk, LANE), F32),
                        pltpu.SemaphoreType.DMA(())],
        compiler_params=_cparams("arbitrary"),
        name="peer_activations",
    )(idx, f, gate, table)


def _peer_out_kernel(idx_ref, act_ref, tab_hbm, o_ref, tab_vmem, splat_a, splat_b, sem, *, tok):
    _load_table_once(tab_hbm, tab_vmem, sem)
    hk = idx_ref.shape[-1]
    low = lax.broadcasted_iota(jnp.int32, (8, LANE), 0) < HALF_SUB
    eye = (lax.broadcasted_iota(jnp.int32, (hk, LANE), 0)
           == lax.broadcasted_iota(jnp.int32, (hk, LANE), 1))
    ones = jnp.ones((LANE, LANE), BF16)

    def make_splat(ref, t):
        for s in range(2):
            diag = jnp.where(eye, act_ref[pl.ds(t + s, 1), :], 0.0)
            d_hi = diag.astype(BF16)
            d_lo = (diag - d_hi.astype(F32)).astype(BF16)
            ref[s] = (jnp.dot(d_hi, ones, preferred_element_type=F32)
                      + jnp.dot(d_lo, ones, preferred_element_type=F32))

    def accumulate(ref, t0):
        for s in range(2):
            t = t0 + s
            acc_lo = [jnp.zeros((8, LANE), F32) for _ in range(2)]
            acc_hi = [jnp.zeros((8, LANE), F32) for _ in range(2)]
            for j in range(hk // 2):
                lo, hi = _unpack(_pair_rows(tab_vmem, idx_ref, t, 2 * j, 2 * j + 1))
                a8 = jnp.where(low, ref[s, pl.ds(2 * j, 1), :], ref[s, pl.ds(2 * j + 1, 1), :])
                acc_lo[j % 2] = acc_lo[j % 2] + a8 * lo
                acc_hi[j % 2] = acc_hi[j % 2] + a8 * hi
            lo8 = acc_lo[0] + acc_lo[1]
            hi8 = acc_hi[0] + acc_hi[1]
            lo8 = lo8 + pltpu.roll(lo8, HALF_SUB, axis=0)
            hi8 = hi8 + pltpu.roll(hi8, HALF_SUB, axis=0)
            o_ref[t] = jnp.where(low, lo8, hi8)

    def body(i, carry):
        t0 = 4 * i
        make_splat(splat_b, t0 + 2)
        accumulate(splat_a, t0)
        make_splat(splat_a, jnp.minimum(t0 + 4, tok - 2))
        accumulate(splat_b, t0 + 2)
        return carry

    make_splat(splat_a, 0)
    lax.fori_loop(0, tok // 4, body, 0)


def peer_outputs(idx, act, table, tok=128):
    n, hk = idx.shape
    kern = functools.partial(_peer_out_kernel, tok=tok)
    return pl.pallas_call(
        kern,
        grid=(n // tok,),
        in_specs=[pl.BlockSpec((tok, hk), lambda i: (i, 0), memory_space=pltpu.SMEM),
                  pl.BlockSpec((tok, hk), lambda i: (i, 0)),
                  pl.BlockSpec(memory_space=pl.ANY)],
        out_specs=pl.BlockSpec((tok, 8, LANE), lambda i: (i, 0, 0)),
        out_shape=jax.ShapeDtypeStruct((n, 8, LANE), F32),
        scratch_shapes=[pltpu.VMEM(table.shape, table.dtype), pltpu.VMEM((2, hk, LANE), F32),
                        pltpu.VMEM((2, hk, LANE), F32), pltpu.SemaphoreType.DMA(())],
        compiler_params=_cparams("arbitrary"),
        name="peer_outputs",
    )(idx, act, table)


def _final_kernel(h_ref, p_ref, g2_ref, gain_ref, o_ref):
    h = h_ref[0] + g2_ref[0] * p_ref[0]
    o_ref[0] = h * lax.rsqrt(jnp.mean(h * h, axis=-1, keepdims=True) + RMS_EPS) * gain_ref[...]


def final_norm_residual(h1, peer_out, g2, gain, tm=512):
    b, s, d = h1.shape
    row = pl.BlockSpec((1, tm, d), lambda i, j: (i, j, 0))
    return pl.pallas_call(
        _final_kernel,
        grid=(b, s // tm),
        in_specs=[row, row, pl.BlockSpec((1, 1, d), lambda i, j: (i, 0, 0)),
                  pl.BlockSpec((1, d), lambda i, j: (0, 0))],
        out_specs=row,
        out_shape=jax.ShapeDtypeStruct((b, s, d), F32),
        compiler_params=_cparams("parallel", "parallel"),
        name="final_norm",
    )(h1, peer_out, g2, gain.reshape(1, d))


def kernel(x, c, ctx, c_ctx, w_mod, b_mod, norm1, w_in, ssd_conv_w, ssd_conv_b, ssd_a_log, ssd_dt_bias, ssd_d, ssd_norm, hy_short_w, hy_short_b, hy_w1, hy_b1, hy_w2, hy_b2, hy_w3, hy_sin_freq, hy_filt_bias, hy_norm, w_out, norm2, peer_wq, peer_subkeys, peer_u, peer_v, final_norm):
    bsz, seq, d = x.shape
    ctx_len = ctx.shape[1]
    w_ssd = SSD_HEADS * HEAD_DIM
    gn = SSD_GROUPS * SSD_STATE
    conv_dim = w_ssd + 2 * gn
    off_dt = w_ssd + conv_dim
    off_hy = off_dt + 2 * SSD_HEADS
    w_hy = (w_in.shape[-1] - off_hy) // 3
    assert bsz == 2 and seq // GRID_W == 2 * FFT_N2 and w_hy == w_ssd

    i = 0
    c_rows = jnp.zeros((8, d), F32).at[:bsz].set(c).at[bsz].set(c_ctx)
    mod = modulation(c_rows, w_mod[i], b_mod[i])
    mod_l = mod[:bsz].reshape(bsz, 1, 6, d)
    mod_c = jnp.broadcast_to(mod[bsz].reshape(1, 1, 6, d), (bsz, 1, 6, d))
    sh1_l, sc1_l, g1_l, sh2_l, sc2_l, g2_l = [mod_l[:, :, j] for j in range(6)]
    sh1_c, sc1_c = mod_c[:, :, 0], mod_c[:, :, 1]

    wi = w_in[i].astype(BF16)
    wz, wx, wh = wi[:, :w_ssd], wi[:, w_ssd:off_dt], wi[:, off_hy:]
    wd = jnp.zeros((d, LANE), BF16).at[:, :2 * SSD_HEADS].set(wi[:, off_dt:off_hy])
    z_l, xbc_l, dt_l, hy_l = in_projection(x, sh1_l, sc1_l, norm1[i], wz, wx, wd, wh, tm=512)
    _, xbc_c, dt_c, _ = in_projection(ctx, sh1_c, sc1_c, norm1[i], wz, wx, wd, wh, tm=ctx_len)

    xa_l = dwconv(xbc_l, ssd_conv_w[i], ssd_conv_b[i], 0, conv_dim, True, BF16, tm=512)
    xa_c = dwconv(xbc_c, ssd_conv_w[i], ssd_conv_b[i], 0, conv_dim, True, BF16, tm=ctx_len)
    pad_row = lambda v: jnp.zeros((1, LANE), F32).at[0, :v.shape[0]].set(v)
    bias_row = pad_row(ssd_dt_bias[i].reshape(-1))
    a_row = pad_row(-jnp.exp(ssd_a_log[i].reshape(-1)))
    dskip = jnp.repeat(ssd_d[i], HEAD_DIM).reshape(1, w_ssd)
    h0 = jnp.zeros((bsz, SSD_GROUPS, SSD_STATE, SSD_HPG * HEAD_DIM), F32)
    _, s_f = ssd_scan(xa_c, dt_c, bias_row, a_row, dskip, h0, rev=False)
    _, s_b = ssd_scan(xa_c, dt_c, bias_row, a_row, dskip, h0, rev=True)
    y_f, _ = ssd_scan(xa_l, dt_l, bias_row, a_row, dskip, s_f, rev=False)
    y_b, _ = ssd_scan(xa_l, dt_l, bias_row, a_row, dskip, s_b, rev=True)

    v = dwconv(hy_l, hy_short_w[i][:, :w_hy], hy_short_b[i][:w_hy], 0, w_hy, False, F32, tm=512)
    x12 = dwconv(hy_l, hy_short_w[i][:, w_hy:], hy_short_b[i][w_hy:], w_hy, 2 * w_hy, False, BF16, tm=512)
    kfull = hyena_filters(seq, hy_w1[i], hy_b1[i], hy_w2[i], hy_b2[i], hy_w3[i], hy_sin_freq[i], w_hy)
    tab_n = _dft_tables(seq, colmajor=False)
    tab_c = _dft_tables(seq, colmajor=True)
    kspec = fft_inner_spectrum(fft_filter_outer(kfull, tab_n[2]), tab_n[3])
    y1 = long_conv(v, kspec, 0, tab_n, colmajor=False)
    zz = hyena_gate(y1, v, x12, hy_filt_bias[i, 0])
    y2 = long_conv(zz, kspec, w_hy, tab_c, colmajor=True)

    h1, f_mod, scores_t = mixer_output(
        x, y_f, y_b, z_l, y2, zz, x12, hy_filt_bias[i, 1], ssd_norm[i], hy_norm[i],
        w_out[i].astype(BF16), g1_l, norm2[i], sh2_l, sc2_l, peer_wq[i].astype(BF16), peer_subkeys[i])

    n_tok = bsz * seq
    idx, gate = peer_topk(scores_t)
    act = peer_activations(idx, f_mod.reshape(n_tok, d), gate, _pack_table(peer_u[i]))
    p_out = peer_outputs(idx, act, _pack_table(peer_v[i]))
    return final_norm_residual(h1, p_out.reshape(bsz, seq, d), g2_l, final_norm)
```

```python
import functools
import math

import jax
import jax.numpy as jnp
from jax import lax
from jax.experimental import pallas as pl
from jax.experimental.pallas import tpu as pltpu

F32 = jnp.float32
BF16 = jnp.bfloat16
HIGHEST = lax.Precision.HIGHEST

LANE = 128
VMEM_LIMIT = 48 * 1024 * 1024

RMS_EPS = 1e-6
GRID_W = 64
HEAD_DIM = 64
SSD_GROUPS = 4
SSD_HPG = 4
SSD_HEADS = SSD_GROUPS * SSD_HPG
SSD_STATE = 128
SSD_CHUNK = 128
SSD_CONV = 5
HY_SHORT = 3
HY_EMB = 33
HY_BANDS = (HY_EMB - 1) // 2
HY_TARGET = 1e-2
HY_MAX_DECAY = math.log(HY_TARGET) / 0.3
HY_MIN_DECAY = math.log(HY_TARGET) / 1.5
PEER_HEADS = 8
PEER_KEYS = 128
PEER_TOPK = 16
PEER_HALF = 128
FFT_N2 = 128
HALO = 16


def _cparams(*sem):
    return pltpu.CompilerParams(dimension_semantics=sem, vmem_limit_bytes=VMEM_LIMIT)


def _silu(x):
    return x * jax.nn.sigmoid(x)


def _mod_kernel(c_ref, w_ref, b_ref, o_ref):
    s = _silu(c_ref[...])
    o_ref[...] = jnp.dot(s, w_ref[...], precision=HIGHEST, preferred_element_type=F32) + b_ref[...]


def modulation(c_rows, w_mod, b_mod):
    d, n = w_mod.shape
    tn = 512
    return pl.pallas_call(
        _mod_kernel,
        grid=(n // tn,),
        in_specs=[pl.BlockSpec((8, d), lambda j: (0, 0)),
                  pl.BlockSpec((d, tn), lambda j: (0, j)),
                  pl.BlockSpec((1, tn), lambda j: (0, j))],
        out_specs=pl.BlockSpec((8, tn), lambda j: (0, j)),
        out_shape=jax.ShapeDtypeStruct((8, n), F32),
        compiler_params=_cparams("arbitrary"),
        name="modulation",
    )(c_rows, w_mod, b_mod.reshape(1, n))


def _inproj_kernel(x_ref, sh_ref, sc_ref, g_ref, wz_ref, wx_ref, wd_ref, wh_ref,
                   z_ref, xbc_ref, dt_ref, hy_ref):
    x = x_ref[0]
    xn = x * lax.rsqrt(jnp.mean(x * x, axis=-1, keepdims=True) + RMS_EPS) * g_ref[...]
    xm = (xn * (1.0 + sc_ref[0]) + sh_ref[0]).astype(BF16)
    z_ref[0] = jnp.dot(xm, wz_ref[...], preferred_element_type=F32).astype(z_ref.dtype)
    xbc_ref[0] = jnp.dot(xm, wx_ref[...], preferred_element_type=F32).astype(xbc_ref.dtype)
    dt_ref[0] = jnp.dot(xm, wd_ref[...], preferred_element_type=F32)
    hy_ref[0] = jnp.dot(xm, wh_ref[...], preferred_element_type=F32).astype(hy_ref.dtype)


def in_projection(x, shift, scale, gain, wz, wx, wd, wh, tm):
    b, s, d = x.shape
    const = lambda w: pl.BlockSpec(w.shape, lambda i, j: (0, 0), pipeline_mode=pl.Buffered(1))
    row = lambda n: pl.BlockSpec((1, tm, n), lambda i, j: (i, j, 0))
    vec = pl.BlockSpec((1, 1, d), lambda i, j: (i, 0, 0))
    return pl.pallas_call(
        _inproj_kernel,
        grid=(b, s // tm),
        in_specs=[row(d), vec, vec, pl.BlockSpec((1, d), lambda i, j: (0, 0)),
                  const(wz), const(wx), const(wd), const(wh)],
        out_specs=[row(wz.shape[1]), row(wx.shape[1]), row(wd.shape[1]), row(wh.shape[1])],
        out_shape=[jax.ShapeDtypeStruct((b, s, wz.shape[1]), BF16),
                   jax.ShapeDtypeStruct((b, s, wx.shape[1]), BF16),
                   jax.ShapeDtypeStruct((b, s, wd.shape[1]), F32),
                   jax.ShapeDtypeStruct((b, s, wh.shape[1]), BF16)],
        compiler_params=_cparams("parallel", "parallel"),
        name="in_projection",
    )(x, shift, scale, gain.reshape(1, d), wz, wx, wd, wh)


def _dwconv_kernel(prev_ref, main_ref, next_ref, w_ref, b_ref, o_ref, ext_ref, *, taps, act, tm, nt):
    i = pl.program_id(1)
    ext_ref[pl.ds(0, HALO), :] = jnp.where(i > 0, prev_ref[0].astype(F32), 0.0)
    ext_ref[pl.ds(HALO, tm), :] = main_ref[0].astype(F32)
    ext_ref[pl.ds(HALO + tm, HALO), :] = jnp.where(i < nt - 1, next_ref[0].astype(F32), 0.0)
    acc = jnp.broadcast_to(b_ref[...], (tm, b_ref.shape[1]))
    for k in range(taps):
        acc = acc + w_ref[pl.ds(k, 1), :] * ext_ref[pl.ds(HALO - taps // 2 + k, tm), :]
    if act:
        acc = _silu(acc)
    o_ref[0] = acc.astype(o_ref.dtype)


def dwconv(u, w, bias, col0, ncols, act, out_dtype, tm, cb=512):
    b, s, _ = u.shape
    taps = w.shape[0]
    nt = s // tm
    hb = tm // HALO
    c0 = col0 // cb
    w8 = jnp.zeros((8, ncols), F32).at[:taps].set(w)
    kern = functools.partial(_dwconv_kernel, taps=taps, act=act, tm=tm, nt=nt)
    return pl.pallas_call(
        kern,
        grid=(b, nt, ncols // cb),
        in_specs=[pl.BlockSpec((1, HALO, cb), lambda bi, i, j: (bi, jnp.maximum(i * hb - 1, 0), j + c0)),
                  pl.BlockSpec((1, tm, cb), lambda bi, i, j: (bi, i, j + c0)),
                  pl.BlockSpec((1, HALO, cb), lambda bi, i, j: (bi, jnp.minimum((i + 1) * hb, s // HALO - 1), j + c0)),
                  pl.BlockSpec((8, cb), lambda bi, i, j: (0, j)),
                  pl.BlockSpec((1, cb), lambda bi, i, j: (0, j))],
        out_specs=pl.BlockSpec((1, tm, cb), lambda bi, i, j: (bi, i, j)),
        out_shape=jax.ShapeDtypeStruct((b, s, ncols), out_dtype),
        scratch_shapes=[pltpu.VMEM((tm + 2 * HALO, cb), F32)],
        compiler_params=_cparams("parallel", "parallel", "parallel"),
        name="dwconv",
    )(u, u, u, w8, bias.reshape(1, ncols))


def _ssd_kernel(x_ref, b_ref, c_ref, dt_ref, bias_ref, a_ref, dskip_ref, h0_ref,
                y_ref, st_ref, *, rev, col0):
    q = SSD_CHUNK
    gw = SSD_HPG * HEAD_DIM

    @pl.when(pl.program_id(1) == 0)
    def _():
        st_ref[...] = h0_ref[...]

    dtp = jax.nn.softplus(dt_ref[0] + bias_ref[...])
    a = dtp * a_ref[...]
    li = lax.broadcasted_iota(jnp.int32, (q, q), 0)
    si = lax.broadcasted_iota(jnp.int32, (q, q), 1)
    tri = (si <= li).astype(F32)
    cum = jnp.dot(tri, a, precision=HIGHEST, preferred_element_type=F32)
    total = cum[q - 1:q, :]
    if rev:
        r = a - cum
        keep = si >= li
        d_out = jnp.exp(total + r)
        d_st = jnp.exp(-r)
    else:
        r = cum
        keep = li >= si
        d_out = jnp.exp(r)
        d_st = jnp.exp(total - r)
    r_t = jnp.transpose(r)
    d_tot = jnp.exp(total)
    lane_head = lax.broadcasted_iota(jnp.int32, (1, gw), 1) // HEAD_DIM

    x = x_ref[0].astype(F32)
    for g in range(SSD_GROUPS):
        cg = c_ref[0, :, g * SSD_STATE:(g + 1) * SSD_STATE]
        bg = b_ref[0, :, g * SSD_STATE:(g + 1) * SSD_STATE]
        cb = lax.dot_general(cg, bg, (((1,), (1,)), ((), ())), preferred_element_type=F32)
        xg = x[:, g * gw:(g + 1) * gw]
        dt_g = jnp.zeros((q, gw), F32)
        dout_g = jnp.zeros((q, gw), F32)
        dst_g = jnp.zeros((q, gw), F32)
        dtot_g = jnp.zeros((1, gw), F32)
        for hh in range(SSD_HPG):
            hc = col0 + g * SSD_HPG + hh
            m = lane_head == hh
            dt_g = jnp.where(m, dtp[:, hc:hc + 1], dt_g)
            dout_g = jnp.where(m, d_out[:, hc:hc + 1], dout_g)
            dst_g = jnp.where(m, d_st[:, hc:hc + 1], dst_g)
            dtot_g = jnp.where(m, d_tot[:, hc:hc + 1], dtot_g)
        xdt = xg * dt_g
        st_old = st_ref[0, g]
        y_g = jnp.dot(cg, st_old.astype(BF16), preferred_element_type=F32) * dout_g
        for hh in range(SSD_HPG):
            hc = col0 + g * SSD_HPG + hh
            diff = r[:, hc:hc + 1] - r_t[hc:hc + 1, :]
            lm = jnp.where(keep, jnp.exp(jnp.minimum(diff, 0.0)), 0.0)
            mh = (cb * lm).astype(BF16)
            xh = jnp.where(lane_head == hh, xdt, 0.0).astype(BF16)
            y_g = y_g + jnp.dot(mh, xh, preferred_element_type=F32)
        if not rev:
            y_g = y_g + xg * dskip_ref[:, g * gw:(g + 1) * gw]
        y_ref[0, :, g * gw:(g + 1) * gw] = y_g.astype(y_ref.dtype)
        upd = lax.dot_general(bg, (xdt * dst_g).astype(BF16), (((0,), (0,)), ((), ())),
                              preferred_element_type=F32)
        st_ref[0, g] = st_old * dtot_g + upd


def ssd_scan(xbc, dt_raw, dt_bias_row, a_row, dskip_row, h0, rev):
    b, s, _ = xbc.shape
    w = SSD_HEADS * HEAD_DIM
    gn = SSD_GROUPS * SSD_STATE
    nc = s // SSD_CHUNK
    cidx = (lambda c: nc - 1 - c) if rev else (lambda c: c)
    kern = functools.partial(_ssd_kernel, rev=rev, col0=SSD_HEADS if rev else 0)
    st_spec = pl.BlockSpec((1, SSD_GROUPS, SSD_STATE, SSD_HPG * HEAD_DIM), lambda bi, c: (bi, 0, 0, 0))
    return pl.pallas_call(
        kern,
        grid=(b, nc),
        in_specs=[pl.BlockSpec((1, SSD_CHUNK, w), lambda bi, c: (bi, cidx(c), 0)),
                  pl.BlockSpec((1, SSD_CHUNK, gn), lambda bi, c: (bi, cidx(c), w // gn)),
                  pl.BlockSpec((1, SSD_CHUNK, gn), lambda bi, c: (bi, cidx(c), w // gn + 1)),
                  pl.BlockSpec((1, SSD_CHUNK, LANE), lambda bi, c: (bi, cidx(c), 0)),
                  pl.BlockSpec((1, LANE), lambda bi, c: (0, 0)),
                  pl.BlockSpec((1, LANE), lambda bi, c: (0, 0)),
                  pl.BlockSpec((1, w), lambda bi, c: (0, 0)),
                  st_spec],
        out_specs=[pl.BlockSpec((1, SSD_CHUNK, w), lambda bi, c: (bi, cidx(c), 0)), st_spec],
        out_shape=[jax.ShapeDtypeStruct((b, s, w), BF16),
                   jax.ShapeDtypeStruct(h0.shape, F32)],
        compiler_params=_cparams("parallel", "arbitrary"),
        name="ssd_scan_rev" if rev else "ssd_scan_fwd",
    )(xbc, xbc, xbc, dt_raw, dt_bias_row, a_row, dskip_row, h0)


def _filter_kernel(z_ref, w1_ref, b1_ref, w2_ref, b2_ref, w3a_ref, w3b_ref, f0_ref, f1_ref, dl_ref, o_ref,
                   *, seqlen):
    z = z_ref[...]
    h = jnp.sin(f0_ref[...] * (jnp.dot(z, w1_ref[...], precision=HIGHEST, preferred_element_type=F32) + b1_ref[...]))
    h = jnp.sin(f1_ref[...] * (jnp.dot(h, w2_ref[...], precision=HIGHEST, preferred_element_type=F32) + b2_ref[...]))
    h_hi = h.astype(BF16)
    h_lo = (h - h_hi.astype(F32)).astype(BF16)
    tl = z.shape[0]
    width = w3a_ref.shape[1]
    n = pl.program_id(0) * tl + lax.broadcasted_iota(jnp.int32, (tl, 1), 0)
    decay = jnp.where(n == seqlen, 0.0, jnp.exp(-z[:, 0:1] * dl_ref[...]))
    for c, w3_ref in enumerate((w3a_ref, w3b_ref)):
        w3 = w3_ref[...]
        w_hi = w3.astype(BF16)
        w_lo = (w3 - w_hi.astype(F32)).astype(BF16)
        y = (jnp.dot(h_hi, w_hi, preferred_element_type=F32) + jnp.dot(h_hi, w_lo, preferred_element_type=F32)
             + jnp.dot(h_lo, w_hi, preferred_element_type=F32))
        o_ref[:, c * width:(c + 1) * width] = y * decay


def hyena_filters(seqlen, w1, b1, w2, b2, w3, sin_freq, width):
    t = jnp.linspace(0.0, 1.0, seqlen, dtype=F32)[:, None]
    w_ang = 2.0 * math.pi * jnp.arange(seqlen, dtype=F32) / seqlen
    bands = jnp.linspace(1e-4, HY_BANDS - 1, HY_BANDS, dtype=F32)
    ang = w_ang[:, None] * bands[None, :]
    zpos = jnp.concatenate([t, jnp.cos(ang), -jnp.sin(ang)], axis=-1)
    zboth = jnp.concatenate([zpos, jnp.zeros_like(zpos[:1]), zpos[:0:-1]], axis=0)
    zpad = jnp.zeros((2 * seqlen, LANE), F32).at[:, :HY_EMB].set(zboth)
    hid = w1.shape[1]
    w1p = jnp.zeros((LANE, hid), F32).at[:HY_EMB].set(w1)
    deltas = jnp.abs(jnp.linspace(HY_MIN_DECAY, HY_MAX_DECAY, width, dtype=F32))
    dl = deltas.reshape(1, width)
    tl = 512
    nfwd = seqlen // tl
    full = lambda a: pl.BlockSpec(a.shape, lambda i: (0, 0))
    vec = pl.BlockSpec((1, hid), lambda i: (0, 0))
    w3_spec = lambda c: pl.BlockSpec((hid, width), lambda i: (0, 2 * c + i // nfwd))
    return pl.pallas_call(
        functools.partial(_filter_kernel, seqlen=seqlen),
        grid=(2 * seqlen // tl,),
        in_specs=[pl.BlockSpec((tl, LANE), lambda i: (i, 0)),
                  full(w1p), vec, full(w2), vec, w3_spec(0), w3_spec(1), vec, vec,
                  pl.BlockSpec((1, width), lambda i: (0, 0))],
        out_specs=pl.BlockSpec((tl, 2 * width), lambda i: (i, 0)),
        out_shape=jax.ShapeDtypeStruct((2 * seqlen, 2 * width), F32),
        compiler_params=_cparams("parallel"),
        name="hyena_filters",
    )(zpad, w1p, b1.reshape(1, hid), w2, b2.reshape(1, hid), w3, w3,
      sin_freq[0].reshape(1, hid), sin_freq[1].reshape(1, hid), dl)


def _dft_tables(seqlen, colmajor):
    n = 2 * seqlen
    n2 = FFT_N2
    n1 = n // n2
    h1 = n1 // 2
    k1 = jnp.arange(n1, dtype=jnp.int32)
    m1 = jnp.arange(h1, dtype=jnp.int32)
    if colmajor:
        m1 = 2 * (m1 % GRID_W) + m1 // GRID_W
    ph = (k1[:, None] * m1[None, :] * n2) % n
    ang = (-2.0 * math.pi / n) * ph.astype(F32)
    wr, wi = jnp.cos(ang), jnp.sin(ang)
    w_fwd = jnp.concatenate([jnp.concatenate([wr, -wi], 1), jnp.concatenate([wi, wr], 1)], 0)
    w_inv = jnp.concatenate([jnp.concatenate([wr.T, wi.T], 1), jnp.concatenate([-wi.T, wr.T], 1)], 0) / n
    mf = jnp.arange(n1, dtype=jnp.int32)
    phf = (k1[:, None] * mf[None, :] * n2) % n
    angf = (-2.0 * math.pi / n) * phf.astype(F32)
    w_flt = jnp.concatenate([jnp.cos(angf), jnp.sin(angf)], 0)
    a2 = jnp.arange(n2, dtype=jnp.int32)
    ph2 = (a2[None, :, None] * a2[None, None, :] * n1 + k1[:, None, None] * a2[None, None, :]) % n
    ang2 = (-2.0 * math.pi / n) * ph2.astype(F32)
    c2, s2 = jnp.cos(ang2), jnp.sin(ang2)
    w2_fwd = jnp.concatenate([jnp.concatenate([c2, -s2], 2), jnp.concatenate([s2, c2], 2)], 1)
    c2t, s2t = jnp.swapaxes(c2, 1, 2), jnp.swapaxes(s2, 1, 2)
    w2_inv = jnp.concatenate([jnp.concatenate([c2t, s2t], 2), jnp.concatenate([-s2t, c2t], 2)], 1)
    return (w_fwd.astype(BF16), w_inv.astype(BF16), w_flt.astype(BF16),
            w2_fwd.astype(BF16), w2_inv.astype(BF16))


def _fft_outer_fwd_kernel(x_ref, w_ref, o_ref, *, colmajor, nb):
    n1 = o_ref.shape[0]
    for j in range(8):
        parts = []
        for bi in range(nb):
            if colmajor:
                parts.append(x_ref[bi, 0, pl.ds(j * GRID_W, GRID_W), :].astype(F32))
                parts.append(x_ref[bi, 1, pl.ds(j * GRID_W, GRID_W), :].astype(F32))
            else:
                parts.append(x_ref[bi, :, j, :])
        xs = jnp.concatenate(parts, axis=0).astype(BF16)
        a = jnp.dot(w_ref[...], xs, preferred_element_type=F32)
        o_ref[:, 0, j, :] = a[:n1]
        o_ref[:, 1, j, :] = a[n1:]


def fft_outer_fwd(x, w, colmajor, cblk=512):
    b, seqlen, c = x.shape
    n2 = FFT_N2
    n1 = 2 * seqlen // n2
    if colmajor:
        xv = x.reshape(b, 2, seqlen // 2, c)
        in_spec = pl.BlockSpec((b, 2, 8 * GRID_W, cblk), lambda g, j: (0, 0, g, j))
    else:
        xv = x.reshape(b, n1 // 2, n2, c)
        in_spec = pl.BlockSpec((b, n1 // 2, 8, cblk), lambda g, j: (0, 0, g, j))
    kern = functools.partial(_fft_outer_fwd_kernel, colmajor=colmajor, nb=b)
    return pl.pallas_call(
        kern,
        grid=(n2 // 8, c // cblk),
        in_specs=[in_spec, pl.BlockSpec(w.shape, lambda g, j: (0, 0))],
        out_specs=pl.BlockSpec((n1, 2, 8, cblk), lambda g, j: (0, 0, g, j)),
        out_shape=jax.ShapeDtypeStruct((n1, 2, n2, c), F32),
        compiler_params=_cparams("parallel", "parallel"),
        name="fft_outer_fwd_cm" if colmajor else "fft_outer_fwd",
    )(xv, w)


def _fft_filter_outer_kernel(k_ref, w_ref, o_ref):
    n1 = o_ref.shape[0]
    for j in range(8):
        a = jnp.dot(w_ref[...], k_ref[:, j, :].astype(BF16), preferred_element_type=F32)
        o_ref[:, 0, j, :] = a[:n1]
        o_ref[:, 1, j, :] = a[n1:]


def fft_filter_outer(k, w, cblk=512):
    n, c = k.shape
    n2 = FFT_N2
    n1 = n // n2
    return pl.pallas_call(
        _fft_filter_outer_kernel,
        grid=(n2 // 8, c // cblk),
        in_specs=[pl.BlockSpec((n1, 8, cblk), lambda g, j: (0, g, j)),
                  pl.BlockSpec(w.shape, lambda g, j: (0, 0))],
        out_specs=pl.BlockSpec((n1, 2, 8, cblk), lambda g, j: (0, 0, g, j)),
        out_shape=jax.ShapeDtypeStruct((n1, 2, n2, c), F32),
        compiler_params=_cparams("parallel", "parallel"),
        name="fft_filter_outer",
    )(k.reshape(n1, n2, c), w)


def _fft_inner_spec_kernel(a_ref, wf_ref, o_ref):
    n2 = FFT_N2
    cb = a_ref.shape[-1]
    for j in range(8):
        xs = a_ref[j].reshape(2 * n2, cb).astype(BF16)
        o_ref[j] = jnp.dot(wf_ref[j], xs, preferred_element_type=F32).reshape(2, n2, cb)


def fft_inner_spectrum(a, w2_fwd, cblk=512):
    n1, _, n2, c = a.shape
    blk = pl.BlockSpec((8, 2, n2, cblk), lambda g, j: (g, 0, 0, j))
    return pl.pallas_call(
        _fft_inner_spec_kernel,
        grid=(n1 // 8, c // cblk),
        in_specs=[blk, pl.BlockSpec((8, 2 * n2, 2 * n2), lambda g, j: (g, 0, 0))],
        out_specs=blk,
        out_shape=jax.ShapeDtypeStruct(a.shape, F32),
        compiler_params=_cparams("parallel", "arbitrary"),
        name="fft_inner_spectrum",
    )(a, w2_fwd)


def _fft_inner_conv_kernel(a_ref, k_ref, wf_ref, wi_ref, o_ref):
    n2 = FFT_N2
    cb = a_ref.shape[-1]
    for j in range(8):
        xs = a_ref[j].reshape(2 * n2, cb).astype(BF16)
        s = jnp.dot(wf_ref[j], xs, preferred_element_type=F32)
        sr, si = s[:n2], s[n2:]
        kr, ki = k_ref[j, 0], k_ref[j, 1]
        y = jnp.concatenate([sr * kr - si * ki, sr * ki + si * kr], axis=0).astype(BF16)
        z = jnp.dot(wi_ref[j], y, preferred_element_type=F32)
        o_ref[:, 0, j, :] = z[:n2]
        o_ref[:, 1, j, :] = z[n2:]


def fft_inner_conv(a, kspec, kcol0, w2_fwd, w2_inv, cblk=512):
    n1, _, n2, c = a.shape
    kc = kcol0 // cblk
    wspec = pl.BlockSpec((8, 2 * n2, 2 * n2), lambda g, j: (g, 0, 0))
    return pl.pallas_call(
        _fft_inner_conv_kernel,
        grid=(n1 // 8, c // cblk),
        in_specs=[pl.BlockSpec((8, 2, n2, cblk), lambda g, j: (g, 0, 0, j)),
                  pl.BlockSpec((8, 2, n2, cblk), lambda g, j: (g, 0, 0, j + kc)),
                  wspec, wspec],
        out_specs=pl.BlockSpec((n2, 2, 8, cblk), lambda g, j: (0, 0, g, j)),
        out_shape=jax.ShapeDtypeStruct((n2, 2, n1, c), F32),
        compiler_params=_cparams("parallel", "arbitrary"),
        name="fft_inner_conv",
    )(a, kspec, w2_fwd, w2_inv)


def _fft_outer_inv_kernel(b_ref, w_ref, o_ref, *, colmajor, nb):
    cb = b_ref.shape[-1]
    n1 = b_ref.shape[2]
    h1 = n1 // 2
    for j in range(8):
        xs = b_ref[j].reshape(2 * n1, cb).astype(BF16)
        y = jnp.dot(w_ref[...], xs, preferred_element_type=F32)
        for bi in range(nb):
            yb = y[bi * h1:(bi + 1) * h1]
            if colmajor:
                o_ref[bi, 0, pl.ds(j * GRID_W, GRID_W), :] = yb[:GRID_W]
                o_ref[bi, 1, pl.ds(j * GRID_W, GRID_W), :] = yb[GRID_W:]
            else:
                o_ref[bi, :, j, :] = yb


def fft_outer_inv(bsp, w, nb, colmajor, cblk=512):
    n2, _, n1, c = bsp.shape
    seqlen = n1 * n2 // 2
    if colmajor:
        out_spec = pl.BlockSpec((nb, 2, 8 * GRID_W, cblk), lambda g, j: (0, 0, g, j))
        out_shape = jax.ShapeDtypeStruct((nb, 2, seqlen // 2, c), F32)
    else:
        out_spec = pl.BlockSpec((nb, n1 // 2, 8, cblk), lambda g, j: (0, 0, g, j))
        out_shape = jax.ShapeDtypeStruct((nb, n1 // 2, n2, c), F32)
    kern = functools.partial(_fft_outer_inv_kernel, colmajor=colmajor, nb=nb)
    y = pl.pallas_call(
        kern,
        grid=(n2 // 8, c // cblk),
        in_specs=[pl.BlockSpec((8, 2, n1, cblk), lambda g, j: (g, 0, 0, j)),
                  pl.BlockSpec(w.shape, lambda g, j: (0, 0))],
        out_specs=out_spec,
        out_shape=out_shape,
        compiler_params=_cparams("parallel", "parallel"),
        name="fft_outer_inv_cm" if colmajor else "fft_outer_inv",
    )(bsp, w)
    return y.reshape(nb, seqlen, c)


def long_conv(u, kspec, kcol0, tables, colmajor):
    w_fwd, w_inv, _, w2_fwd, w2_inv = tables
    a = fft_outer_fwd(u, w_fwd, colmajor)
    bsp = fft_inner_conv(a, kspec, kcol0, w2_fwd, w2_inv)
    return fft_outer_inv(bsp, w_inv, u.shape[0], colmajor)


def _hy_gate_kernel(y_ref, v_ref, x1_ref, fb_ref, o_ref):
    o_ref[0] = x1_ref[0].astype(F32) * (y_ref[0] + v_ref[0] * fb_ref[...])


def hyena_gate(y, v, x12, fbias, tm=512, cb=512):
    b, s, c = y.shape
    blk = lambda: pl.BlockSpec((1, tm, cb), lambda bi, i, j: (bi, i, j))
    return pl.pallas_call(
        _hy_gate_kernel,
        grid=(b, s // tm, c // cb),
        in_specs=[blk(), blk(), blk(), pl.BlockSpec((1, cb), lambda bi, i, j: (0, j))],
        out_specs=blk(),
        out_shape=jax.ShapeDtypeStruct((b, s, c), F32),
        compiler_params=_cparams("parallel", "parallel", "parallel"),
        name="hyena_gate",
    )(y, v, x12, fbias.reshape(1, c))


def _mix_kernel(x_ref, yf_ref, yb_ref, z_ref, y2_ref, zz_ref, x12_ref,
                fb_ref, gs_ref, gh_ref, mh_ref, wo_ref, g1_ref, n2_ref, sh2_ref, sc2_ref,
                wq_ref, sk_ref, h1_ref, f_ref, st_ref):
    w = yf_ref.shape[-1]
    gw = w // SSD_GROUPS
    ys = (yf_ref[0].astype(F32) + yb_ref[0].astype(F32)) * _silu(z_ref[0].astype(F32))
    parts = []
    for g in range(SSD_GROUPS):
        blk = ys[:, g * gw:(g + 1) * gw]
        ms = jnp.mean(blk * blk, axis=-1, keepdims=True)
        parts.append(blk * lax.rsqrt(ms + RMS_EPS))
    o_s = jnp.concatenate(parts, axis=1) * gs_ref[...]
    yh = x12_ref[0].astype(F32) * (y2_ref[0] + zz_ref[0] * fb_ref[...])
    parts = []
    for c in range(w // LANE):
        blk = yh[:, c * LANE:(c + 1) * LANE]
        sq = blk * blk
        hi = sq.astype(BF16)
        lo = (sq - hi.astype(F32)).astype(BF16)
        ms = (jnp.dot(hi, mh_ref[...], preferred_element_type=F32)
              + jnp.dot(lo, mh_ref[...], preferred_element_type=F32)) * (1.0 / HEAD_DIM)
        parts.append(blk * lax.rsqrt(ms + RMS_EPS))
    o_h = jnp.concatenate(parts, axis=1) * gh_ref[...]
    mix = (jnp.dot(o_s.astype(BF16), wo_ref[pl.ds(0, w), :], preferred_element_type=F32)
           + jnp.dot(o_h.astype(BF16), wo_ref[pl.ds(w, w), :], preferred_element_type=F32))
    h1 = x_ref[0] + g1_ref[0] * mix
    h1_ref[0] = h1
    hn = h1 * lax.rsqrt(jnp.mean(h1 * h1, axis=-1, keepdims=True) + RMS_EPS) * n2_ref[...]
    f = hn * (1.0 + sc2_ref[0]) + sh2_ref[0]
    f_ref[0] = f
    qv =jnp.dot(f.astype(BF16), wq_ref[...], preferred_element_type=F32).astype(BF16)
    for blk in range(2 * PEER_HEADS):
        qb = qv[:, blk * PEER_HALF:(blk + 1) * PEER_HALF]
        st_ref[blk] = lax.dot_general(sk_ref[blk % 2], qb, (((1,), (1,)), ((), ())),
                                      preferred_element_type=F32)


def mixer_output(x, y_f, y_b, z, y2, zz, x12, fbias, g_ssd, g_hy, w_out, g1, norm2, sh2, sc2, wq, subkeys, tm=256):
    b, s, d = x.shape
    w = y_f.shape[-1]
    nblk = 2 * PEER_HEADS
    lane = jnp.arange(LANE)
    m_h = (lane[:, None] // HEAD_DIM == lane[None, :] // HEAD_DIM).astype(BF16)
    row = lambda n, c0=0: pl.BlockSpec((1, tm, n), lambda i, j: (i, j, c0))
    vec = lambda n: pl.BlockSpec((1, 1, n), lambda i, j: (i, 0, 0))
    const = lambda a: pl.BlockSpec(a.shape, lambda i, j: (0,) * a.ndim, pipeline_mode=pl.Buffered(1))
    sk = subkeys.astype(BF16)
    return pl.pallas_call(
        _mix_kernel,
        grid=(b, s // tm),
        in_specs=[row(d), row(w), row(w), row(w), row(w), row(w), row(w, 1),
                  pl.BlockSpec((1, w), lambda i, j: (0, 0)),
                  pl.BlockSpec((1, w), lambda i, j: (0, 0)),
                  pl.BlockSpec((1, w), lambda i, j: (0, 0)),
                  const(m_h), const(w_out), vec(d),
                  pl.BlockSpec((1, d), lambda i, j: (0, 0)), vec(d), vec(d),
                  const(wq), const(sk)],
        out_specs=[row(d), row(d),
                   pl.BlockSpec((nblk, PEER_KEYS, tm), lambda i, j: (0, 0, i * (s // tm) + j))],
        out_shape=[jax.ShapeDtypeStruct((b, s, d), F32),
                   jax.ShapeDtypeStruct((b, s, d), F32),
                   jax.ShapeDtypeStruct((nblk, PEER_KEYS, b * s), F32)],
        compiler_params=_cparams("parallel", "parallel"),
        name="mixer_output",
    )(x, y_f, y_b, z, y2, zz, x12, fbias.reshape(1, w), g_ssd.reshape(1, w), g_hy.reshape(1, w),
      m_h, w_out, g1, norm2.reshape(1, d), sh2, sc2, wq, sk)


def _topk_rows(x, riota, k):
    t = x.shape[1]
    slot = lax.broadcasted_iota(jnp.int32, (k, t), 0)
    vals = jnp.zeros((k, t), F32)
    idxs = jnp.zeros((k, t), F32)
    big = jnp.float32(1e9)
    for it in range(k):
        m = jnp.max(x, axis=0, keepdims=True)
        sel = jnp.min(jnp.where(x == m, riota, big), axis=0, keepdims=True)
        vals = jnp.where(slot == it, m, vals)
        idxs = jnp.where(slot == it, sel, idxs)
        x = jnp.where(riota == sel, -jnp.inf, x)
    return vals, idxs


def _pair_candidates(v1, v2, k):
    assert k == 16
    t = v1.shape[1]
    b16 = lax.broadcasted_iota(jnp.int32, (k, t), 0)
    b8 = lax.broadcasted_iota(jnp.int32, (8, t), 0)
    vals = [v1[0:1, :] + v2]
    ids = [b16.astype(F32)]
    for a in range(1, 8):
        vals.append(jnp.where(b8 < k // (a + 1), v1[a:a + 1, :] + v2[0:8, :], -jnp.inf))
        ids.append((b8 + a * k).astype(F32))
    vals.append(v1[8:16, :] + v2[0:1, :])
    ids.append(((b8 + 8) * k).astype(F32))
    return jnp.concatenate(vals, axis=0), jnp.concatenate(ids, axis=0)


def _topk_kernel(s_ref, idx_ref, gate_ref, idx_t, gate_t):
    k = PEER_TOPK
    t = s_ref.shape[-1]
    key_iota = lax.broadcasted_iota(jnp.int32, (PEER_KEYS, t), 0).astype(F32)
    r16 = lax.broadcasted_iota(jnp.int32, (k, t), 0).astype(F32)
    slot = lax.broadcasted_iota(jnp.int32, (k, t), 0)

    def subkey_topk(h):
        v1, i1 = _topk_rows(s_ref[2 * h], key_iota, k)
        v2, i2 = _topk_rows(s_ref[2 * h + 1], key_iota, k)
        return v1, i1, v2, i2

    def finish(h, v1, i1, v2, i2):
        cand, cid = _pair_candidates(v1, v2, k)
        sc, ci = _topk_rows(cand, cid, k)
        e = jnp.exp(sc - sc[0:1, :])
        ih = jnp.zeros((k, t), F32)
        for j in range(k):
            cj = ci[j:j + 1, :]
            a = jnp.floor(cj * (1.0 / k))
            bb = cj - a * k
            e1 = jnp.sum(jnp.where(r16 == a, i1, 0.0), axis=0, keepdims=True)
            e2 = jnp.sum(jnp.where(r16 == bb, i2, 0.0), axis=0, keepdims=True)
            ih = jnp.where(slot == j, e1 * PEER_KEYS + e2, ih)
        row0 = pl.multiple_of(h * k, k)
        idx_t[pl.ds(row0, k), :] = ih
        gate_t[pl.ds(row0, k), :] = e / jnp.sum(e, axis=0, keepdims=True)

    def body(h, prev):
        finish(h - 1, *prev)
        return subkey_topk(h)

    last = lax.fori_loop(1, PEER_HEADS, body, subkey_topk(0))
    finish(PEER_HEADS - 1, *last)
    idx_ref[...] = jnp.transpose(idx_t[...]).astype(jnp.int32)
    gate_ref[...] = jnp.transpose(gate_t[...])


def peer_topk(scores_t, tt=128):
    nblk, nk, n = scores_t.shape
    hk = PEER_HEADS * PEER_TOPK
    return pl.pallas_call(
        _topk_kernel,
        grid=(n // tt,),
        in_specs=[pl.BlockSpec((nblk, nk, tt), lambda i: (0, 0, i))],
        out_specs=[pl.BlockSpec((tt, hk), lambda i: (i, 0)), pl.BlockSpec((tt, hk), lambda i: (i, 0))],
        out_shape=[jax.ShapeDtypeStruct((n, hk), jnp.int32), jax.ShapeDtypeStruct((n, hk), F32)],
        scratch_shapes=[pltpu.VMEM((hk, tt), F32), pltpu.VMEM((hk, tt), F32)],
        compiler_params=_cparams("parallel"),
        name="peer_topk",
    )(scores_t)


HALF_SUB = 4


def _pack_table(tab):
    e, d = tab.shape
    bits = lax.bitcast_convert_type(tab.astype(BF16), jnp.uint16).astype(jnp.uint32)
    return ((bits[:, d // 2:] << 16) | bits[:, :d // 2]).reshape(e, d // (2 * LANE), LANE)


def _pair_rows(tab_vmem, idx_ref, t, ka, kb):
    row = idx_ref.at[t]
    return jnp.concatenate([tab_vmem[row[ka]], tab_vmem[row[kb]]], axis=0)


def _unpack(wd):
    lo = pltpu.bitcast(wd << 16, F32)
    hi = pltpu.bitcast(wd & jnp.uint32(0xFFFF0000), F32)
    return lo, hi


def _load_table_once(tab_hbm, tab_vmem, sem):
    @pl.when(pl.program_id(0) == 0)
    def _():
        cp = pltpu.make_async_copy(tab_hbm, tab_vmem, sem)
        cp.start()
        cp.wait()


def _peer_act_kernel(idx_ref, f_ref, gate_ref, tab_hbm, act_ref, tab_vmem, part_ref, sem, *, tok):
    _load_table_once(tab_hbm, tab_vmem, sem)
    hk = gate_ref.shape[-1]
    sub = lax.broadcasted_iota(jnp.int32, (8, LANE), 0)
    low = sub < HALF_SUB
    eye = (lax.broadcasted_iota(jnp.int32, (hk, LANE), 0)
           == lax.broadcasted_iota(jnp.int32, (hk, LANE), 1))

    def split_f(t):
        frow = f_ref[pl.ds(t, 1), :]
        chunk = lambda c: frow[:, c * LANE:(c + 1) * LANE]
        s4 = sub & (HALF_SUB - 1)
        pick = lambda c0: jnp.where(s4 == 0, chunk(c0), jnp.where(s4 == 1, chunk(c0 + 1),
                                    jnp.where(s4 == 2, chunk(c0 + 2), chunk(c0 + 3))))
        return pick(0), pick(HALF_SUB)

    def partial_sums(i, carry):
        toks = (2 * i, 2 * i + 1)
        fs = [split_f(t) for t in toks]
        for g in range(hk // 8):
            for t, (flo, fhi) in zip(toks, fs):
                merged = None
                for j in range(HALF_SUB):
                    lo, hi = _unpack(_pair_rows(tab_vmem, idx_ref, t, 8 * g + j, 8 * g + HALF_SUB + j))
                    p = lo * flo + hi * fhi
                    p = p + pltpu.roll(p, 6, axis=0)
                    p = p + pltpu.roll(p, 7, axis=0)
                    if j:
                        p = pltpu.roll(p, j, axis=0)
                    merged = p if merged is None else jnp.where((sub & (HALF_SUB - 1)) == j, p, merged)
                part_ref[t, pl.ds(8 * g, 8), :] = merged
        return carry

    def lane_sums(i, carry):
        rows = []
        for s8 in range(8):
            s = jnp.sum(part_ref[8 * i + s8], axis=1, keepdims=True)
            rows.append(jnp.sum(jnp.where(eye, s, 0.0), axis=0, keepdims=True))
        act_ref[pl.ds(pl.multiple_of(8 * i, 8), 8), :] = jnp.concatenate(rows, axis=0)
        return carry

    lax.fori_loop(0, tok // 2, partial_sums, 0)
    lax.fori_loop(0, tok // 8, lane_sums, 0)
    pre = act_ref[...]
    act_ref[...] = 0.5 * pre * (1.0 + lax.erf(pre * (1.0 / math.sqrt(2.0)))) * gate_ref[...]


def peer_activations(idx, f, gate, table, tok=128):
    n, hk = idx.shape
    kern = functools.partial(_peer_act_kernel, tok=tok)
    return pl.pallas_call(
        kern,
        grid=(n // tok,),
        in_specs=[pl.BlockSpec((tok, hk), lambda i: (i, 0), memory_space=pltpu.SMEM),
                  pl.BlockSpec((tok, f.shape[1]), lambda i: (i, 0)),
                  pl.BlockSpec((tok, hk), lambda i: (i, 0)),
                  pl.BlockSpec(memory_space=pl.ANY)],
        out_specs=pl.BlockSpec((tok, hk), lambda i: (i, 0)),
        out_shape=jax.ShapeDtypeStruct((n, hk), F32),
        scratch_shapes=[pltpu.VMEM(table.shape, table.dtype), pltpu.VMEM((tok, hk, LANE), F32),
                        pltpu.SemaphoreType.DMA(())],
        compiler_params=_cparams("arbitrary"),
        name="peer_activations",
    )(idx, f, gate, table)


def _peer_out_kernel(idx_ref, act_ref, tab_hbm, o_ref, tab_vmem, splat_a, splat_b, sem, *, tok):
    _load_table_once(tab_hbm, tab_vmem, sem)
    hk = idx_ref.shape[-1]
    low = lax.broadcasted_iota(jnp.int32, (8, LANE), 0) < HALF_SUB
    eye = (lax.broadcasted_iota(jnp.int32, (hk, LANE), 0)
           == lax.broadcasted_iota(jnp.int32, (hk, LANE), 1))
    ones = jnp.ones((LANE, LANE), BF16)

    def make_splat(ref, t):
        for s in range(2):
            diag = jnp.where(eye, act_ref[pl.ds(t + s, 1), :], 0.0)
            d_hi = diag.astype(BF16)
            d_lo = (diag - d_hi.astype(F32)).astype(BF16)
            ref[s] = (jnp.dot(d_hi, ones, preferred_element_type=F32)
                      + jnp.dot(d_lo, ones, preferred_element_type=F32))

    def accumulate(ref, t0):
        for s in range(2):
            t = t0 + s
            acc_lo = [jnp.zeros((8, LANE), F32) for _ in range(2)]
            acc_hi = [jnp.zeros((8, LANE), F32) for _ in range(2)]
            for j in range(hk // 2):
                lo, hi = _unpack(_pair_rows(tab_vmem, idx_ref, t, 2 * j, 2 * j + 1))
                a8 = jnp.where(low, ref[s, pl.ds(2 * j, 1), :], ref[s, pl.ds(2 * j + 1, 1), :])
                acc_lo[j % 2] = acc_lo[j % 2] + a8 * lo
                acc_hi[j % 2] = acc_hi[j % 2] + a8 * hi
            lo8 = acc_lo[0] + acc_lo[1]
            hi8 = acc_hi[0] + acc_hi[1]
            lo8 = lo8 + pltpu.roll(lo8, HALF_SUB, axis=0)
            hi8 = hi8 + pltpu.roll(hi8, HALF_SUB, axis=0)
            tile = jnp.where(low, lo8, hi8)
            o_ref[pl.ds(t, 1), :] = jnp.concatenate([tile[c:c + 1, :] for c in range(8)], axis=1)

    def body(i, carry):
        t0 = 4 * i
        make_splat(splat_b, t0 + 2)
        accumulate(splat_a, t0)
        make_splat(splat_a, jnp.minimum(t0 + 4, tok - 2))
        accumulate(splat_b, t0 + 2)
        return carry

    make_splat(splat_a, 0)
    lax.fori_loop(0, tok // 4, body, 0)


def peer_outputs(idx, act, table, tok=128):
    n, hk = idx.shape
    kern = functools.partial(_peer_out_kernel, tok=tok)
    return pl.pallas_call(
        kern,
        grid=(n // tok,),
        in_specs=[pl.BlockSpec((tok, hk), lambda i: (i, 0), memory_space=pltpu.SMEM),
                  pl.BlockSpec((tok, hk), lambda i: (i, 0)),
                  pl.BlockSpec(memory_space=pl.ANY)],
        out_specs=pl.BlockSpec((tok, 8 * LANE), lambda i: (i, 0)),
        out_shape=jax.ShapeDtypeStruct((n, 8 * LANE), F32),
        scratch_shapes=[pltpu.VMEM(table.shape, table.dtype), pltpu.VMEM((2, hk, LANE), F32),
                        pltpu.VMEM((2, hk, LANE), F32), pltpu.SemaphoreType.DMA(())],
        compiler_params=_cparams("arbitrary"),
        name="peer_outputs",
    )(idx, act, table)


def _final_kernel(h_ref, p_ref, g2_ref, gain_ref, o_ref):
    h = h_ref[0] + g2_ref[0] * p_ref[0]
    o_ref[0] = h * lax.rsqrt(jnp.mean(h * h, axis=-1, keepdims=True) + RMS_EPS) * gain_ref[...]


def final_norm_residual(h1, peer_out, g2, gain, tm=512):
    b, s, d = h1.shape
    row = pl.BlockSpec((1, tm, d), lambda i, j: (i, j, 0))
    return pl.pallas_call(
        _final_kernel,
        grid=(b, s // tm),
        in_specs=[row, row, pl.BlockSpec((1, 1, d), lambda i, j: (i, 0, 0)),
                  pl.BlockSpec((1, d), lambda i, j: (0, 0))],
        out_specs=row,
        out_shape=jax.ShapeDtypeStruct((b, s, d), F32),
        compiler_params=_cparams("parallel", "parallel"),
        name="final_norm",
    )(h1, peer_out, g2, gain.reshape(1, d))


def kernel(x, c, ctx, c_ctx, w_mod, b_mod, norm1, w_in, ssd_conv_w, ssd_conv_b, ssd_a_log, ssd_dt_bias, ssd_d, ssd_norm, hy_short_w, hy_short_b, hy_w1, hy_b1, hy_w2, hy_b2, hy_w3, hy_sin_freq, hy_filt_bias, hy_norm, w_out, norm2, peer_wq, peer_subkeys, peer_u, peer_v, final_norm):
    bsz, seq, d = x.shape
    ctx_len = ctx.shape[1]
    w_ssd = SSD_HEADS * HEAD_DIM
    gn = SSD_GROUPS * SSD_STATE
    conv_dim = w_ssd + 2 * gn
    off_dt = w_ssd + conv_dim
    off_hy = off_dt + 2 * SSD_HEADS
    w_hy = (w_in.shape[-1] - off_hy) // 3
    assert bsz == 2 and seq // GRID_W == 2 * FFT_N2 and w_hy == w_ssd

    i = 0
    c_rows = jnp.zeros((8, d), F32).at[:bsz].set(c).at[bsz].set(c_ctx)
    mod = modulation(c_rows, w_mod[i], b_mod[i])
    mod_l = mod[:bsz].reshape(bsz, 1, 6, d)
    mod_c = jnp.broadcast_to(mod[bsz].reshape(1, 1, 6, d), (bsz, 1, 6, d))
    sh1_l, sc1_l, g1_l, sh2_l, sc2_l, g2_l = [mod_l[:, :, j] for j in range(6)]
    sh1_c, sc1_c = mod_c[:, :, 0], mod_c[:, :, 1]

    wi = w_in[i].astype(BF16)
    wz, wx, wh = wi[:, :w_ssd], wi[:, w_ssd:off_dt], wi[:, off_hy:]
    wd = jnp.zeros((d, LANE), BF16).at[:, :2 * SSD_HEADS].set(wi[:, off_dt:off_hy])
    z_l, xbc_l, dt_l, hy_l = in_projection(x, sh1_l, sc1_l, norm1[i], wz, wx, wd, wh, tm=512)
    _, xbc_c, dt_c, _ = in_projection(ctx, sh1_c, sc1_c, norm1[i], wz, wx, wd, wh, tm=ctx_len)

    xa_l = dwconv(xbc_l, ssd_conv_w[i], ssd_conv_b[i], 0, conv_dim, True, BF16, tm=512)
    xa_c = dwconv(xbc_c, ssd_conv_w[i], ssd_conv_b[i], 0, conv_dim, True, BF16, tm=ctx_len)
    pad_row = lambda v: jnp.zeros((1, LANE), F32).at[0, :v.shape[0]].set(v)
    bias_row = pad_row(ssd_dt_bias[i].reshape(-1))
    a_row = pad_row(-jnp.exp(ssd_a_log[i].reshape(-1)))
    dskip = jnp.repeat(ssd_d[i], HEAD_DIM).reshape(1, w_ssd)
    h0 = jnp.zeros((bsz, SSD_GROUPS, SSD_STATE, SSD_HPG * HEAD_DIM), F32)
    _, s_f = ssd_scan(xa_c, dt_c, bias_row, a_row, dskip, h0, rev=False)
    _, s_b = ssd_scan(xa_c, dt_c, bias_row, a_row, dskip, h0, rev=True)
    y_f, _ = ssd_scan(xa_l, dt_l, bias_row, a_row, dskip, s_f, rev=False)
    y_b, _ = ssd_scan(xa_l, dt_l, bias_row, a_row, dskip, s_b, rev=True)

    v = dwconv(hy_l, hy_short_w[i][:, :w_hy], hy_short_b[i][:w_hy], 0, w_hy, False, F32, tm=512)
    x12 = dwconv(hy_l, hy_short_w[i][:, w_hy:], hy_short_b[i][w_hy:], w_hy, 2 * w_hy, False, BF16, tm=512)
    kfull = hyena_filters(seq, hy_w1[i], hy_b1[i], hy_w2[i], hy_b2[i], hy_w3[i], hy_sin_freq[i], w_hy)
    tab_n = _dft_tables(seq, colmajor=False)
    tab_c = _dft_tables(seq, colmajor=True)
    kspec = fft_inner_spectrum(fft_filter_outer(kfull, tab_n[2]), tab_n[3])
    y1 = long_conv(v, kspec, 0, tab_n, colmajor=False)
    zz = hyena_gate(y1, v, x12, hy_filt_bias[i, 0])
    y2 = long_conv(zz, kspec, w_hy, tab_c, colmajor=True)

    h1, f_mod, scores_t = mixer_output(
        x, y_f, y_b, z_l, y2, zz, x12, hy_filt_bias[i, 1], ssd_norm[i], hy_norm[i],
        w_out[i].astype(BF16), g1_l, norm2[i], sh2_l, sc2_l, peer_wq[i].astype(BF16), peer_subkeys[i])

    n_tok = bsz * seq
    idx, gate = peer_topk(scores_t)
    act = peer_activations(idx, f_mod.reshape(n_tok, d), gate, _pack_table(peer_u[i]))
    p_out = peer_outputs(idx, act, _pack_table(peer_v[i]))
    return final_norm_residual(h1, p_out.reshape(bsz, seq, d), g2_l, final_norm)
```
